```python
import math
import jax, jax.numpy as jnp
from jax import lax
import numpy as np

D_MODEL = 2048
BATCH = 2
SEQ = 4096
DEPTH = 4
DEC_BATCH = 8
DEC_SEQ = 4
PAST_LEN = 16384
PAGE_SIZE = 128

N_MIXERS = 3
D_FF = 4 * D_MODEL
EPS = 1e-6
NEG = -1e30

FOX_HEADS = 16
FOX_HEAD_DIM = D_MODEL // FOX_HEADS
FOX_IN_DIM = 4 * D_MODEL + FOX_HEADS
Q_BLOCK = 128
FGATE_BIAS = 3.0

GDN_QK_HEADS = 16
GDN_V_HEADS = 32
GDN_HEAD_DIM = D_MODEL // 16
GDN_KEY_DIM = GDN_QK_HEADS * GDN_HEAD_DIM
GDN_VAL_DIM = GDN_V_HEADS * GDN_HEAD_DIM
GDN_CONV_DIM = 2 * GDN_KEY_DIM + GDN_VAL_DIM
GDN_IN_DIM = GDN_CONV_DIM + GDN_VAL_DIM + 2 * GDN_V_HEADS
CONV_WIDTH = 4
GDN_CHUNK = 64

MLSTM_HEADS = 8
MLSTM_QK_DIM = D_MODEL // 2 // MLSTM_HEADS
MLSTM_V_DIM = D_MODEL // MLSTM_HEADS
MLSTM_IN_DIM = 2 * MLSTM_HEADS * MLSTM_QK_DIM + MLSTM_HEADS * MLSTM_V_DIM + D_MODEL + 2 * MLSTM_HEADS
MLSTM_CHUNK = 64
GATE_SOFTCAP = 15.0

N_FOX = (DEPTH + 2) // 3
N_GDN = (DEPTH + 1) // 3
N_MLSTM = DEPTH // 3

kernel_name = 'hybrid_fox_gdn_mlstm_decode_step'


def rmsnorm(x, g):
    xf = x.astype(jnp.float32)
    y = xf * lax.rsqrt(jnp.mean(xf * xf, axis=-1, keepdims=True) + EPS)
    return (y * g.astype(jnp.float32)).astype(x.dtype)


def l2norm(x):
    return x * lax.rsqrt(jnp.sum(x * x, axis=-1, keepdims=True) + EPS)


def sq_relu_mlp(x, w_up, w_down):
    h = jnp.maximum(x @ w_up, 0)
    return (h * h) @ w_down


def pad_time(x, pad, value=0.0):
    if pad == 0:
        return x
    widths = [(0, 0)] * x.ndim
    widths[1] = (0, pad)
    return jnp.pad(x, widths, constant_values=value)


def to_chunks(x, L):
    B, T, H = x.shape[:3]
    x = x.reshape((B, T // L, L, H) + x.shape[3:])
    return jnp.moveaxis(x, (1, 3), (0, 2))


def from_chunks(x, T):
    x = jnp.moveaxis(x, (0, 2), (1, 3))
    return x.reshape((x.shape[0], -1) + x.shape[3:])[:, :T]


def fox_project(h, w_in, b_f, q_gain, k_gain):
    B, T, _ = h.shape
    z = h @ w_in
    q, k, v, g = jnp.split(z[..., :4 * D_MODEL], 4, axis=-1)
    hs = (B, T, FOX_HEADS, FOX_HEAD_DIM)
    q = rmsnorm(q.reshape(hs), q_gain)
    k = rmsnorm(k.reshape(hs), k_gain)
    logf = jax.nn.log_sigmoid((z[..., 4 * D_MODEL:] + b_f).astype(jnp.float32))
    return q, k, v.reshape(hs), g, logf


def fox_attend(q, k, v, cq, ck, q_pos, k_pos):
    s = jnp.einsum('bqhd,bkhd->bhqk', q, k).astype(jnp.float32) * (FOX_HEAD_DIM ** -0.5)
    bias = jnp.swapaxes(cq, 1, 2)[..., :, None] - jnp.swapaxes(ck, 1, 2)[..., None, :]
    s = jnp.where(k_pos[None, :] <= q_pos[:, None], s + bias, -jnp.inf)
    p = jax.nn.softmax(s, axis=-1).astype(v.dtype)
    return jnp.einsum('bhqk,bkhd->bqhd', p, v)


def fox_prompt(h, w_in, b_f, q_gain, k_gain, w_out):
    B, T, _ = h.shape
    q, k, v, g, logf = fox_project(h, w_in, b_f, q_gain, k_gain)
    cum = jnp.cumsum(logf, axis=1)
    nb = T // Q_BLOCK
    pos = jnp.arange(T)
    blocks = (jnp.moveaxis(q.reshape(B, nb, Q_BLOCK, FOX_HEADS, FOX_HEAD_DIM), 1, 0),
              jnp.moveaxis(cum.reshape(B, nb, Q_BLOCK, FOX_HEADS), 1, 0),
              pos.reshape(nb, Q_BLOCK))
    o = lax.map(lambda blk: fox_attend(blk[0], k, v, blk[1], cum, blk[2], pos), blocks)
    o = jnp.moveaxis(o, 0, 1).reshape(B, T, D_MODEL)
    out = (o * jax.nn.sigmoid(g)) @ w_out
    return out.astype(h.dtype), k, v, logf


def fox_sample(h, cache_k, cache_v, cache_logf, page_table, w_in, b_f, q_gain, k_gain, w_out):
    B, T, _ = h.shape
    P = page_table.shape[1] * PAGE_SIZE
    q, k, v, g, logf = fox_project(h, w_in, b_f, q_gain, k_gain)
    past_k = cache_k[page_table].reshape(B, P, FOX_HEADS, FOX_HEAD_DIM).astype(k.dtype)
    past_v = cache_v[page_table].reshape(B, P, FOX_HEADS, FOX_HEAD_DIM).astype(v.dtype)
    past_l = cache_logf[page_table].reshape(B, P, FOX_HEADS).astype(jnp.float32)
    cum = jnp.cumsum(jnp.concatenate([past_l, logf], axis=1), axis=1)
    o = fox_attend(q, jnp.concatenate([past_k, k], axis=1), jnp.concatenate([past_v, v], axis=1),
                   cum[:, P:], cum, P + jnp.arange(T), jnp.arange(P + T))
    out = (o.reshape(B, T, D_MODEL) * jax.nn.sigmoid(g)) @ w_out
    return out.astype(h.dtype), k, v, logf


def causal_conv(u, buf, w):
    full = jnp.concatenate([buf.astype(u.dtype), u], axis=1)
    out = lax.conv_general_dilated(full, w[:, None, :].astype(u.dtype), window_strides=(1,), padding='VALID',
                                   dimension_numbers=('NWC', 'WIO', 'NWC'), feature_group_count=u.shape[-1])
    return out, full[:, u.shape[1]:]


def gated_delta_chunked(q, k, v, beta, g, S0):
    T = q.shape[1]
    L = min(GDN_CHUNK, T)
    pad = (-T) % L
    q, k, v, beta, g = [to_chunks(pad_time(a, pad), L) for a in (q, k, v, beta, g)]
    G = jnp.cumsum(g, axis=-1)
    causal = jnp.tril(jnp.ones((L, L), bool))
    strict = jnp.tril(jnp.ones((L, L), bool), -1)
    decay = jnp.exp(jnp.where(causal, G[..., :, None] - G[..., None, :], -jnp.inf))
    kb = k * beta[..., None]
    M = jnp.where(strict, jnp.einsum('nbhid,nbhjd->nbhij', kb, k) * decay, 0.0)
    rhs = jnp.concatenate([v * beta[..., None], kb * jnp.exp(G)[..., None]], axis=-1)
    sol = lax.linalg.triangular_solve(M + jnp.eye(L, dtype=M.dtype), rhs, left_side=True, lower=True,
                                      unit_diagonal=True)
    dv = v.shape[-1]
    u, w = sol[..., :dv], sol[..., dv:]
    qk = jnp.where(causal, jnp.einsum('nbhid,nbhjd->nbhij', q, k) * decay, 0.0)

    def step(S, xs):
        q_c, k_c, u_c, w_c, qk_c, G_c = xs
        v_new = u_c - w_c @ S
        o = (q_c * jnp.exp(G_c)[..., None]) @ S + qk_c @ v_new
        g_last = G_c[..., -1]
        k_dec = k_c * jnp.exp(g_last[..., None] - G_c)[..., None]
        S = S * jnp.exp(g_last)[..., None, None] + jnp.einsum('bhld,bhle->bhde', k_dec, v_new)
        return S, o

    S, o = lax.scan(step, S0, (q, k, u, w, qk, G))
    return from_chunks(o, T), S


def gdn_mix(h, conv_buf, S0, w_in, conv_w, A_log, dt_bias, norm_g, w_out):
    B, T, _ = h.shape
    z = h @ w_in
    qkv, new_buf = causal_conv(z[..., :GDN_CONV_DIM], conv_buf, conv_w)
    qkv = jax.nn.silu(qkv.astype(jnp.float32))
    q = l2norm(qkv[..., :GDN_KEY_DIM].reshape(B, T, GDN_QK_HEADS, GDN_HEAD_DIM)) * (GDN_HEAD_DIM ** -0.5)
    k = l2norm(qkv[..., GDN_KEY_DIM:2 * GDN_KEY_DIM].reshape(B, T, GDN_QK_HEADS, GDN_HEAD_DIM))
    v = qkv[..., 2 * GDN_KEY_DIM:].reshape(B, T, GDN_V_HEADS, GDN_HEAD_DIM)
    rep = GDN_V_HEADS // GDN_QK_HEADS
    q = jnp.repeat(q, rep, axis=2)
    k = jnp.repeat(k, rep, axis=2)
    gates = z[..., GDN_CONV_DIM + GDN_VAL_DIM:].astype(jnp.float32)
    beta = jax.nn.sigmoid(gates[..., :GDN_V_HEADS])
    g = -jnp.exp(A_log.astype(jnp.float32)) * jax.nn.softplus(gates[..., GDN_V_HEADS:] + dt_bias)
    o, S = gated_delta_chunked(q, k, v, beta, g, S0.astype(jnp.float32))
    zg = z[..., GDN_CONV_DIM:GDN_CONV_DIM + GDN_VAL_DIM].reshape(B, T, GDN_V_HEADS, GDN_HEAD_DIM)
    o = rmsnorm(o, norm_g) * jax.nn.silu(zg.astype(jnp.float32))
    out = o.reshape(B, T, GDN_VAL_DIM).astype(h.dtype) @ w_out
    return out, new_buf, S


def mlstm_chunked(q, k, v, i_gate, logf, C0, n0, m0):
    T = q.shape[1]
    L = min(MLSTM_CHUNK, T)
    pad = (-T) % L
    q, k, v, logf = [to_chunks(pad_time(a, pad), L) for a in (q, k, v, logf)]
    i_gate = to_chunks(pad_time(i_gate, pad, NEG), L)
    a = jnp.cumsum(logf, axis=-1)
    causal = jnp.tril(jnp.ones((L, L), bool))
    D = jnp.where(causal, a[..., :, None] - a[..., None, :] + i_gate[..., None, :], NEG)
    d_max = jnp.max(D, axis=-1)
    qk = jnp.einsum('nbhid,nbhjd->nbhij', q, k)

    def step(carry, xs):
        C, n, m = carry
        q_c, k_c, v_c, a_c, i_c, D_c, dmax_c, qk_c = xs
        inter = a_c + m[..., None]
        m_row = jnp.maximum(inter, dmax_c)
        w_inter = jnp.exp(inter - m_row)
        S = qk_c * jnp.exp(D_c - m_row[..., None])
        num = w_inter[..., None] * (q_c @ C) + S @ v_c
        den = w_inter * jnp.einsum('bhld,bhd->bhl', q_c, n) + jnp.sum(S, axis=-1)
        hh = num / jnp.maximum(jnp.abs(den), jnp.exp(-m_row))[..., None]
        m_new = m_row[..., -1]
        a_last = a_c[..., -1]
        carry_scale = jnp.exp(a_last + m - m_new)
        kw = k_c * jnp.exp(a_last[..., None] - a_c + i_c - m_new[..., None])[..., None]
        C = carry_scale[..., None, None] * C + jnp.einsum('bhld,bhle->bhde', kw, v_c)
        n = carry_scale[..., None] * n + jnp.sum(kw, axis=-2)
        return (C, n, m_new), hh

    (C, n, m), hh = lax.scan(step, (C0, n0, m0), (q, k, v, a, i_gate, D, d_max, qk))
    return from_chunks(hh, T), C, n, m


def mlstm_mix(h, C0, n0, m0, w_in, b_i, b_f, norm_g, w_out):
    B, T, _ = h.shape
    z = (h @ w_in).astype(jnp.float32)
    qk_w = MLSTM_HEADS * MLSTM_QK_DIM
    v_w = MLSTM_HEADS * MLSTM_V_DIM
    q = z[..., :qk_w].reshape(B, T, MLSTM_HEADS, MLSTM_QK_DIM)
    k = z[..., qk_w:2 * qk_w].reshape(B, T, MLSTM_HEADS, MLSTM_QK_DIM) * (MLSTM_QK_DIM ** -0.5)
    v = z[..., 2 * qk_w:2 * qk_w + v_w].reshape(B, T, MLSTM_HEADS, MLSTM_V_DIM)
    o_gate = z[..., 2 * qk_w + v_w:2 * qk_w + v_w + D_MODEL]
    gates = z[..., 2 * qk_w + v_w + D_MODEL:]
    i_gate = GATE_SOFTCAP * jnp.tanh((gates[..., :MLSTM_HEADS] + b_i) / GATE_SOFTCAP)
    logf = jax.nn.log_sigmoid(GATE_SOFTCAP * jnp.tanh((gates[..., MLSTM_HEADS:] + b_f) / GATE_SOFTCAP))
    hh, C, n, m = mlstm_chunked(q, k, v, i_gate, logf, C0.astype(jnp.float32), n0.astype(jnp.float32),
                                m0.astype(jnp.float32))
    hh = rmsnorm(hh, norm_g.reshape(MLSTM_HEADS, MLSTM_V_DIM)).reshape(B, T, D_MODEL)
    out = (jax.nn.sigmoid(o_gate) * hh).astype(h.dtype) @ w_out
    return out, C, n, m


def setup_inputs(seed: int = 0) -> dict:
    key = jax.random.key(seed)
    ks = iter(jax.random.split(key, 48))
    f32 = jnp.float32

    def nrm(shape, scale=1.0):
        return scale * jax.random.normal(next(ks), shape, f32)

    def gain(shape):
        return 1.0 + nrm(shape, 0.02)

    n_pages = PAST_LEN // PAGE_SIZE
    used = DEC_BATCH * n_pages
    n_pool = used + max(1, used // 4)
    page_table = jax.random.permutation(next(ks), n_pool)[:used].reshape(DEC_BATCH, n_pages).astype(jnp.int32)
    kv_shape = (n_pool, PAGE_SIZE, FOX_HEADS, FOX_HEAD_DIM)
    lf_shape = (n_pool, PAGE_SIZE, FOX_HEADS)

    x_prompt = nrm((BATCH, SEQ, D_MODEL))
    x_sample = nrm((DEC_BATCH, DEC_SEQ, D_MODEL))
    cache_k_l0 = nrm(kv_shape)
    cache_v_l0 = nrm(kv_shape)
    cache_logf_l0 = jax.nn.log_sigmoid(FGATE_BIAS + nrm(lf_shape))
    state_conv_l1 = nrm((DEC_BATCH, CONV_WIDTH - 1, GDN_CONV_DIM))
    state_ssm_l1 = nrm((DEC_BATCH, GDN_V_HEADS, GDN_HEAD_DIM, GDN_HEAD_DIM), 0.1)
    state_C_l2 = nrm((DEC_BATCH, MLSTM_HEADS, MLSTM_QK_DIM, MLSTM_V_DIM), 0.1)
    state_n_l2 = nrm((DEC_BATCH, MLSTM_HEADS, MLSTM_QK_DIM), 0.1)
    state_m_l2 = nrm((DEC_BATCH, MLSTM_HEADS))
    cache_k_l3 = nrm(kv_shape)
    cache_v_l3 = nrm(kv_shape)
    cache_logf_l3 = jax.nn.log_sigmoid(FGATE_BIAS + nrm(lf_shape))

    dt = jnp.exp(jax.random.uniform(next(ks), (N_GDN, GDN_V_HEADS), f32, math.log(1e-3), math.log(1e-1)))
    return {
        'x_prompt': x_prompt, 'x_sample': x_sample,
        'cache_k_l0': cache_k_l0, 'cache_v_l0': cache_v_l0, 'cache_logf_l0': cache_logf_l0,
        'state_conv_l1': state_conv_l1, 'state_ssm_l1': state_ssm_l1,
        'state_C_l2': state_C_l2, 'state_n_l2': state_n_l2, 'state_m_l2': state_m_l2,
        'cache_k_l3': cache_k_l3, 'cache_v_l3': cache_v_l3, 'cache_logf_l3': cache_logf_l3,
        'page_table': page_table,
        'norm_mix': gain((DEPTH, D_MODEL)),
        'norm_mlp': gain((DEPTH, D_MODEL)),
        'w_up': nrm((DEPTH, D_MODEL, D_FF), D_MODEL ** -0.5),
        'w_down': nrm((DEPTH, D_FF, D_MODEL), D_FF ** -0.5),
        'fox_w_in': nrm((N_FOX, D_MODEL, FOX_IN_DIM), D_MODEL ** -0.5),
        'fox_b_f': FGATE_BIAS + nrm((N_FOX, FOX_HEADS), 0.1),
        'fox_q_norm': gain((N_FOX, FOX_HEAD_DIM)),
        'fox_k_norm': gain((N_FOX, FOX_HEAD_DIM)),
        'fox_w_out': nrm((N_FOX, D_MODEL, D_MODEL), D_MODEL ** -0.5),
        'gdn_w_in': nrm((N_GDN, D_MODEL, GDN_IN_DIM), D_MODEL ** -0.5),
        'gdn_conv_w': nrm((N_GDN, CONV_WIDTH, GDN_CONV_DIM), CONV_WIDTH ** -0.5),
        'gdn_A_log': jnp.log(jax.random.uniform(next(ks), (N_GDN, GDN_V_HEADS), f32, 1.0, 16.0)),
        'gdn_dt_bias': dt + jnp.log(-jnp.expm1(-dt)),
        'gdn_norm': gain((N_GDN, GDN_HEAD_DIM)),
        'gdn_w_out': nrm((N_GDN, GDN_VAL_DIM, D_MODEL), GDN_VAL_DIM ** -0.5),
        'mlstm_w_in': nrm((N_MLSTM, D_MODEL, MLSTM_IN_DIM), D_MODEL ** -0.5),
        'mlstm_b_i': nrm((N_MLSTM, MLSTM_HEADS), 0.1),
        'mlstm_b_f': FGATE_BIAS + nrm((N_MLSTM, MLSTM_HEADS), 0.1),
        'mlstm_norm': gain((N_MLSTM, D_MODEL)),
        'mlstm_w_out': nrm((N_MLSTM, D_MODEL, D_MODEL), D_MODEL ** -0.5),
    }


def reference(x_prompt, x_sample, cache_k_l0, cache_v_l0, cache_logf_l0, state_conv_l1, state_ssm_l1,
              state_C_l2, state_n_l2, state_m_l2, cache_k_l3, cache_v_l3, cache_logf_l3, page_table,
              norm_mix, norm_mlp, w_up, w_down,
              fox_w_in, fox_b_f, fox_q_norm, fox_k_norm, fox_w_out,
              gdn_w_in, gdn_conv_w, gdn_A_log, gdn_dt_bias, gdn_norm, gdn_w_out,
              mlstm_w_in, mlstm_b_i, mlstm_b_f, mlstm_norm, mlstm_w_out):
    fox_cache = {0: (cache_k_l0, cache_v_l0, cache_logf_l0), 3: (cache_k_l3, cache_v_l3, cache_logf_l3)}
    gdn_state = {1: (state_conv_l1, state_ssm_l1)}
    mlstm_state = {2: (state_C_l2, state_n_l2, state_m_l2)}
    yp, ys = x_prompt, x_sample
    Bp = x_prompt.shape[0]
    new = {}
    for i in range(DEPTH):
        kind, j = i % N_MIXERS, i // N_MIXERS
        hp = rmsnorm(yp, norm_mix[i])
        hs = rmsnorm(ys, norm_mix[i])
        if kind == 0:
            wts = (fox_w_in[j], fox_b_f[j], fox_q_norm[j], fox_k_norm[j], fox_w_out[j])
            mp, kp, vp, lp = fox_prompt(hp, *wts)
            ck, cv, cl = fox_cache[i]
            ms, kq, vq, lq = fox_sample(hs, ck, cv, cl, page_table, *wts)
            new[f'k_prompt_l{i}'], new[f'v_prompt_l{i}'], new[f'logf_prompt_l{i}'] = kp, vp, lp
            new[f'k_sample_l{i}'], new[f'v_sample_l{i}'], new[f'logf_sample_l{i}'] = kq, vq, lq
        elif kind == 1:
            wts = (gdn_w_in[j], gdn_conv_w[j], gdn_A_log[j], gdn_dt_bias[j], gdn_norm[j], gdn_w_out[j])
            buf0 = jnp.zeros((Bp, CONV_WIDTH - 1, GDN_CONV_DIM), hp.dtype)
            S0 = jnp.zeros((Bp, GDN_V_HEADS, GDN_HEAD_DIM, GDN_HEAD_DIM), jnp.float32)
            mp, bp, Sp = gdn_mix(hp, buf0, S0, *wts)
            sbuf, sS = gdn_state[i]
            ms, bq, Sq = gdn_mix(hs, sbuf, sS, *wts)
            new[f'conv_prompt_l{i}'], new[f'ssm_prompt_l{i}'] = bp, Sp
            new[f'conv_sample_l{i}'], new[f'ssm_sample_l{i}'] = bq, Sq
        else:
            wts = (mlstm_w_in[j], mlstm_b_i[j], mlstm_b_f[j], mlstm_norm[j], mlstm_w_out[j])
            C0 = jnp.zeros((Bp, MLSTM_HEADS, MLSTM_QK_DIM, MLSTM_V_DIM), jnp.float32)
            n0 = jnp.zeros((Bp, MLSTM_HEADS, MLSTM_QK_DIM), jnp.float32)
            m0 = jnp.zeros((Bp, MLSTM_HEADS), jnp.float32)
            mp, Cp, np_, mp_state = mlstm_mix(hp, C0, n0, m0, *wts)
            sC, sn, sm = mlstm_state[i]
            ms, Cq, nq, mq = mlstm_mix(hs, sC, sn, sm, *wts)
            new[f'C_prompt_l{i}'], new[f'n_prompt_l{i}'], new[f'm_prompt_l{i}'] = Cp, np_, mp_state
            new[f'C_sample_l{i}'], new[f'n_sample_l{i}'], new[f'm_sample_l{i}'] = Cq, nq, mq
        yp = yp + mp
        ys = ys + ms
        yp = yp + sq_relu_mlp(rmsnorm(yp, norm_mlp[i]), w_up[i], w_down[i])
        ys = ys + sq_relu_mlp(rmsnorm(ys, norm_mlp[i]), w_up[i], w_down[i])
    return (yp, ys,
            new['k_prompt_l0'], new['v_prompt_l0'], new['logf_prompt_l0'],
            new['k_sample_l0'], new['v_sample_l0'], new['logf_sample_l0'],
            new['conv_prompt_l1'], new['ssm_prompt_l1'], new['conv_sample_l1'], new['ssm_sample_l1'],
            new['C_prompt_l2'], new['n_prompt_l2'], new['m_prompt_l2'],
            new['C_sample_l2'], new['n_sample_l2'], new['m_sample_l2'],
            new['k_prompt_l3'], new['v_prompt_l3'], new['logf_prompt_l3'],
            new['k_sample_l3'], new['v_sample_l3'], new['logf_sample_l3'])
```

```python
import functools

import jax
import jax.numpy as jnp
from jax import lax
from jax.experimental import pallas as pl
from jax.experimental.pallas import tpu as pltpu

F32 = jnp.float32
BF16 = jnp.bfloat16
EPS = 1e-6
NEG = -1e30
LANES = 128
CHUNK = 64
VMEM_LIMIT = 56 * 1024 * 1024
GATE_SOFTCAP = 15.0
_NT = (((1,), (1,)), ((), ()))
_TN = (((0,), (0,)), ((), ()))


def _params(*sem):
    return pltpu.CompilerParams(dimension_semantics=sem, vmem_limit_bytes=VMEM_LIMIT)


def _mm(a, b):
    return jnp.dot(a.astype(BF16), b.astype(BF16), preferred_element_type=F32)


def _mm_nt(a, b):
    return lax.dot_general(a.astype(BF16), b.astype(BF16), _NT, preferred_element_type=F32)


def _mm_tn(a, b):
    return lax.dot_general(a.astype(BF16), b.astype(BF16), _TN, preferred_element_type=F32)


def _split(a):
    hi = a.astype(BF16)
    return hi, (a - hi.astype(F32)).astype(BF16)


def _mm3(a, b):
    ah, al = _split(a)
    bh, bl = _split(b)
    dot = lambda x, y: jnp.dot(x, y, preferred_element_type=F32)
    return dot(ah, bh) + (dot(ah, bl) + dot(al, bh))


def _sigmoid(x):
    return 1.0 / (1.0 + jnp.exp(-x))


def _softplus(x):
    return jnp.maximum(x, 0.0) + jnp.log(1.0 + jnp.exp(-jnp.abs(x)))


def _log_sigmoid(x):
    return -_softplus(-x)


def _rms(x, gain):
    return x * lax.rsqrt(jnp.mean(x * x, axis=-1, keepdims=True) + EPS) * gain


def _iota(shape, axis):
    return lax.broadcasted_iota(jnp.int32, shape, axis)


def _lane_cumsum(x):
    n = x.shape[-1]
    lane = _iota(x.shape, x.ndim - 1)
    s = 1
    while s < n:
        x = x + jnp.where(lane >= s, pltpu.roll(x, s, x.ndim - 1), 0.0)
        s *= 2
    return x


def _chunk_row_cumsum(x, chunk):
    pos = _iota(x.shape, 0) % chunk
    s = 1
    while s < chunk:
        x = x + jnp.where(pos >= s, pltpu.roll(x, s, 0), 0.0)
        s *= 2
    return x


def _pick_lane(block, lane_idx):
    return jnp.sum(jnp.where(_iota(block.shape, 1) == lane_idx, block, 0.0), axis=1, keepdims=True)


def _proj_kernel(x_ref, g_ref, w_ref, wt_ref, z_ref, zt_ref, xn_ref):
    @pl.when(pl.program_id(1) == 0)
    def _():
        xn = _rms(x_ref[...], g_ref[...]).astype(BF16)
        xn_ref[...] = xn
        zt_ref[...] = jnp.dot(xn, wt_ref[...].astype(BF16), preferred_element_type=F32)

    z_ref[...] = jnp.dot(xn_ref[...], w_ref[...].astype(BF16), preferred_element_type=F32)


def _proj(x, gain, w_main, w_tail, tm, tn):
    M, D = x.shape
    N = w_main.shape[1]
    NT_ = w_tail.shape[1]
    return pl.pallas_call(
        _proj_kernel,
        grid=(M // tm, N // tn),
        in_specs=[pl.BlockSpec((tm, D), lambda i, j: (i, 0)),
                  pl.BlockSpec((1, D), lambda i, j: (0, 0)),
                  pl.BlockSpec((D, tn), lambda i, j: (0, j)),
                  pl.BlockSpec((D, NT_), lambda i, j: (0, 0))],
        out_specs=[pl.BlockSpec((tm, tn), lambda i, j: (i, j)),
                   pl.BlockSpec((tm, NT_), lambda i, j: (i, 0))],
        out_shape=[jax.ShapeDtypeStruct((M, N), F32), jax.ShapeDtypeStruct((M, NT_), F32)],
        scratch_shapes=[pltpu.VMEM((tm, D), BF16)],
        compiler_params=_params("parallel", "arbitrary"),
        name="proj",
    )(x, gain, w_main, w_tail)


def _oproj_kernel(a_ref, w_ref, r_ref, o_ref):
    o_ref[...] = r_ref[...] + jnp.dot(a_ref[...], w_ref[...].astype(BF16), preferred_element_type=F32)


def _oproj(a, w, resid, tm, tn):
    M, K = a.shape
    N = w.shape[1]
    return pl.pallas_call(
        _oproj_kernel,
        grid=(M // tm, N // tn),
        in_specs=[pl.BlockSpec((tm, K), lambda i, j: (i, 0)),
                  pl.BlockSpec((K, tn), lambda i, j: (0, j)),
                  pl.BlockSpec((tm, tn), lambda i, j: (i, j))],
        out_specs=pl.BlockSpec((tm, tn), lambda i, j: (i, j)),
        out_shape=jax.ShapeDtypeStruct((M, N), F32),
        compiler_params=_params("parallel", "arbitrary"),
        name="oproj",
    )(a, w, resid)


def _mlp_kernel(x_ref, g_ref, wu_ref, wd_ref, o_ref, xn_ref):
    @pl.when(pl.program_id(1) == 0)
    def _():
        x = x_ref[...]
        xn_ref[...] = _rms(x, g_ref[...]).astype(BF16)
        o_ref[...] = x

    h = jnp.dot(xn_ref[...], wu_ref[...].astype(BF16), preferred_element_type=F32)
    h = jnp.maximum(h, 0.0)
    o_ref[...] += jnp.dot((h * h).astype(BF16), wd_ref[...].astype(BF16), preferred_element_type=F32)


def _mlp(x, gain, w_up, w_down, tm, tf):
    M, D = x.shape
    Fdim = w_up.shape[1]
    return pl.pallas_call(
        _mlp_kernel,
        grid=(M // tm, Fdim // tf),
        in_specs=[pl.BlockSpec((tm, D), lambda i, j: (i, 0)),
                  pl.BlockSpec((1, D), lambda i, j: (0, 0)),
                  pl.BlockSpec((D, tf), lambda i, j: (0, j)),
                  pl.BlockSpec((tf, D), lambda i, j: (j, 0))],
        out_specs=pl.BlockSpec((tm, D), lambda i, j: (i, 0)),
        out_shape=jax.ShapeDtypeStruct((M, D), F32),
        scratch_shapes=[pltpu.VMEM((tm, D), BF16)],
        compiler_params=_params("parallel", "arbitrary"),
        name="mlp",
    )(x, gain, w_up, w_down)


def _fox_proj_kernel(x_ref, g_ref, w_ref, wt_ref, bt_ref, qg_ref, kg_ref,
                     q_ref, k_ref, kb_ref, v_ref, vb_ref, gate_ref, lf_ref, xn_ref, *, nseg, dh):
    j = pl.program_id(1)

    @pl.when(j == 0)
    def _():
        xn = _rms(x_ref[...], g_ref[...]).astype(BF16)
        xn_ref[...] = xn
        zt = jnp.dot(xn, wt_ref[...].astype(BF16), preferred_element_type=F32) + bt_ref[...]
        lf_ref[...] = _log_sigmoid(zt)

    acc = jnp.dot(xn_ref[...], w_ref[...].astype(BF16), preferred_element_type=F32)
    heads = acc.shape[1] // dh
    seg = j // nseg

    @pl.when(seg == 0)
    def _():
        for h in range(heads):
            sl = slice(h * dh, (h + 1) * dh)
            q_ref[:, sl] = _rms(acc[:, sl], qg_ref[...]).astype(BF16)

    @pl.when(seg == 1)
    def _():
        for h in range(heads):
            sl = slice(h * dh, (h + 1) * dh)
            kn = _rms(acc[:, sl], kg_ref[...])
            k_ref[:, sl] = kn
            kb_ref[:, sl] = kn.astype(BF16)

    @pl.when(seg == 2)
    def _():
        v_ref[...] = acc
        vb_ref[...] = acc.astype(BF16)

    @pl.when(seg == 3)
    def _():
        gate_ref[...] = acc


def _fox_proj(x, gain, w_main, w_tail, b_tail, q_gain, k_gain, tm, tn):
    M, D = x.shape
    dh = q_gain.shape[1]
    nseg = D // tn

    def seg_map(s):
        return lambda i, j: (i, jnp.clip(j - s * nseg, 0, nseg - 1))

    row = lambda i, j: (i, 0)
    const = lambda i, j: (0, 0)
    kern = functools.partial(_fox_proj_kernel, nseg=nseg, dh=dh)
    return pl.pallas_call(
        kern,
        grid=(M // tm, 4 * nseg),
        in_specs=[pl.BlockSpec((tm, D), row),
                  pl.BlockSpec((1, D), const),
                  pl.BlockSpec((D, tn), lambda i, j: (0, j)),
                  pl.BlockSpec((D, LANES), const),
                  pl.BlockSpec((1, LANES), const),
                  pl.BlockSpec((1, dh), const),
                  pl.BlockSpec((1, dh), const)],
        out_specs=[pl.BlockSpec((tm, tn), seg_map(0)),
                   pl.BlockSpec((tm, tn), seg_map(1)),
                   pl.BlockSpec((tm, tn), seg_map(1)),
                   pl.BlockSpec((tm, tn), seg_map(2)),
                   pl.BlockSpec((tm, tn), seg_map(2)),
                   pl.BlockSpec((tm, tn), seg_map(3)),
                   pl.BlockSpec((tm, LANES), row)],
        out_shape=[jax.ShapeDtypeStruct((M, D), BF16),
                   jax.ShapeDtypeStruct((M, D), F32),
                   jax.ShapeDtypeStruct((M, D), BF16),
                   jax.ShapeDtypeStruct((M, D), F32),
                   jax.ShapeDtypeStruct((M, D), BF16),
                   jax.ShapeDtypeStruct((M, D), F32),
                   jax.ShapeDtypeStruct((M, LANES), F32)],
        scratch_shapes=[pltpu.VMEM((tm, D), BF16)],
        compiler_params=_params("parallel", "arbitrary"),
        name="fox_proj",
    )(x, gain, w_main, w_tail, b_tail, q_gain, k_gain)


def _cumsum_kernel(x_ref, o_ref):
    o_ref[...] = _lane_cumsum(x_ref[...])


def _time_cumsum(x):
    B, H, T = x.shape
    return pl.pallas_call(
        _cumsum_kernel,
        grid=(B,),
        in_specs=[pl.BlockSpec((None, H, T), lambda b: (b, 0, 0))],
        out_specs=pl.BlockSpec((None, H, T), lambda b: (b, 0, 0)),
        out_shape=jax.ShapeDtypeStruct((B, H, T), F32),
        compiler_params=_params("parallel"),
        name="time_cumsum",
    )(x)


def _fox_attn_kernel(q_ref, k_ref, v_ref, ck_ref, g_ref, o_ref, m_ref, l_ref, acc_ref, *, blk, scale):
    qi = pl.program_id(2)
    ki = pl.program_id(3)

    @pl.when(ki == 0)
    def _():
        m_ref[...] = jnp.full(m_ref.shape, NEG, F32)
        l_ref[...] = jnp.zeros(l_ref.shape, F32)
        acc_ref[...] = jnp.zeros(acc_ref.shape, F32)

    @pl.when(ki <= qi)
    def _():
        s = lax.dot_general(q_ref[...], k_ref[...], _NT, preferred_element_type=F32) * scale - ck_ref[...]
        row = qi * blk + _iota(s.shape, 0)
        col = ki * blk + _iota(s.shape, 1)
        s = jnp.where(col <= row, s, NEG)
        m_prev = m_ref[...]
        m_new = jnp.maximum(m_prev, jnp.max(s, axis=1, keepdims=True))
        alpha = jnp.exp(m_prev - m_new)
        p = jnp.exp(s - m_new)
        l_ref[...] = alpha * l_ref[...] + jnp.sum(p, axis=1, keepdims=True)
        acc_ref[...] = alpha * acc_ref[...] + jnp.dot(p.astype(BF16), v_ref[...], preferred_element_type=F32)
        m_ref[...] = m_new

    @pl.when(ki == pl.num_programs(3) - 1)
    def _():
        o = acc_ref[...] / l_ref[...] * _sigmoid(g_ref[...])
        o_ref[...] = o.astype(BF16)


def _fox_attn(q, kb, vb, cum_t, gate, B, T, H, dh, blk):
    M, D = q.shape
    nb = T // blk
    kern = functools.partial(_fox_attn_kernel, blk=blk, scale=dh ** -0.5)
    qmap = lambda b, h, qi, ki: (b * nb + qi, h)
    kmap = lambda b, h, qi, ki: (b * nb + jnp.minimum(ki, qi), h)
    return pl.pallas_call(
        kern,
        grid=(B, H, nb, nb),
        in_specs=[pl.BlockSpec((blk, dh), qmap),
                  pl.BlockSpec((blk, dh), kmap),
                  pl.BlockSpec((blk, dh), kmap),
                  pl.BlockSpec((None, 1, blk), lambda b, h, qi, ki: (b * H + h, 0, jnp.minimum(ki, qi))),
                  pl.BlockSpec((blk, dh), qmap)],
        out_specs=pl.BlockSpec((blk, dh), qmap),
        out_shape=jax.ShapeDtypeStruct((M, D), BF16),
        scratch_shapes=[pltpu.VMEM((blk, 1), F32), pltpu.VMEM((blk, 1), F32), pltpu.VMEM((blk, dh), F32)],
        compiler_params=_params("parallel", "parallel", "parallel", "arbitrary"),
        name="fox_attn",
    )(q, kb, vb, cum_t, gate)


def _fox_decode_kernel(pt_ref, q_ref, kp_ref, vp_ref, lfp_ref, kn_ref, vn_ref, lfn_ref, g_ref, o_ref,
                       qbd_ref, m_ref, l_ref, acc_ref, coff_ref, *, npages, nh, nt, dh, scale):
    p_id = pl.program_id(1)
    R = nt * nh
    D = nh * dh

    def head_mask():
        return (_iota((R, D), 0) % nh) == (_iota((R, D), 1) // dh)

    @pl.when(p_id == 0)
    def _():
        q = q_ref[...]
        rows = jnp.concatenate([jnp.broadcast_to(q[t:t + 1, :], (nh, D)) for t in range(nt)], axis=0)
        qbd_ref[...] = jnp.where(head_mask(), rows, 0.0).astype(BF16)
        m_ref[...] = jnp.full(m_ref.shape, NEG, F32)
        l_ref[...] = jnp.zeros(l_ref.shape, F32)
        acc_ref[...] = jnp.zeros(acc_ref.shape, F32)
        coff_ref[...] = jnp.zeros(coff_ref.shape, F32)

    def step(k, v, lf_t, causal):
        tot = coff_ref[...] + _lane_cumsum(lf_t)
        bias = jnp.concatenate([tot] * nt, axis=0)
        s = lax.dot_general(qbd_ref[...], k.astype(BF16), _NT, preferred_element_type=F32) * scale - bias
        if causal:
            s = jnp.where(_iota(s.shape, 1) <= _iota(s.shape, 0) // nh, s, NEG)
        m_prev = m_ref[...]
        m_new = jnp.maximum(m_prev, jnp.max(s, axis=1, keepdims=True))
        alpha = jnp.exp(m_prev - m_new)
        p = jnp.exp(s - m_new)
        l_ref[...] = alpha * l_ref[...] + jnp.sum(p, axis=1, keepdims=True)
        acc_ref[...] = alpha * acc_ref[...] + jnp.dot(p.astype(BF16), v.astype(BF16), preferred_element_type=F32)
        m_ref[...] = m_new
        coff_ref[...] = tot[:, tot.shape[1] - 1:]

    @pl.when(p_id < npages)
    def _():
        step(kp_ref[...], vp_ref[...], lfp_ref[...], False)

    @pl.when(p_id == npages)
    def _():
        step(kn_ref[...], vn_ref[...], lfn_ref[...], True)
        o_full = jnp.where(head_mask(), acc_ref[...] / l_ref[...], 0.0)
        o = jnp.concatenate([jnp.sum(o_full[t * nh:(t + 1) * nh, :], axis=0, keepdims=True) for t in range(nt)],
                            axis=0)
        o_ref[...] = o * _sigmoid(g_ref[...])


def _fox_decode(page_table, q, cache_k, cache_v, cache_lf_t, k_new, v_new, lf_new_t, gate, nh, dh):
    B, nt, D = q.shape
    npages = page_table.shape[1]
    page = cache_k.shape[1]
    kern = functools.partial(_fox_decode_kernel, npages=npages, nh=nh, nt=nt, dh=dh, scale=dh ** -0.5)
    pmap = lambda b, p, pt: (pt[b, jnp.minimum(p, npages - 1)], 0, 0)
    bmap = lambda b, p, pt: (b, 0, 0)
    R = nt * nh
    grid_spec = pltpu.PrefetchScalarGridSpec(
        num_scalar_prefetch=1,
        grid=(B, npages + 1),
        in_specs=[pl.BlockSpec((None, nt, D), bmap),
                  pl.BlockSpec((None, page, D), pmap),
                  pl.BlockSpec((None, page, D), pmap),
                  pl.BlockSpec((None, nh, page), pmap),
                  pl.BlockSpec((None, page, D), bmap),
                  pl.BlockSpec((None, page, D), bmap),
                  pl.BlockSpec((None, nh, page), bmap),
                  pl.BlockSpec((None, nt, D), bmap)],
        out_specs=pl.BlockSpec((None, nt, D), bmap),
        scratch_shapes=[pltpu.VMEM((R, D), BF16), pltpu.VMEM((R, 1), F32), pltpu.VMEM((R, 1), F32),
                        pltpu.VMEM((R, D), F32), pltpu.VMEM((nh, 1), F32)])
    return pl.pallas_call(
        kern,
        grid_spec=grid_spec,
        out_shape=jax.ShapeDtypeStruct((B, nt, D), F32),
        compiler_params=_params("parallel", "arbitrary"),
        name="fox_decode",
    )(page_table, q, cache_k, cache_v, cache_lf_t, k_new, v_new, lf_new_t, gate)


def _pad_cols(a, n):
    return jnp.pad(a, ((0, 0), (0, n - a.shape[1])))


def _row_tile(m, pref):
    return pref if m % pref == 0 else m


def _fox_layer(yp, ys, cache_k, cache_v, cache_logf, page_table, gain, w_in, b_f, q_gain, k_gain, w_out):
    B, T, D = yp.shape
    Bs, Ts, _ = ys.shape
    H = b_f.shape[0]
    dh = D // H
    w_main = w_in[:, :4 * D]
    w_tail = _pad_cols(w_in[:, 4 * D:], LANES)
    b_tail = _pad_cols(b_f.reshape(1, H), LANES)
    gain = gain.reshape(1, D)
    qg = q_gain.reshape(1, dh)
    kg = k_gain.reshape(1, dh)

    x2 = yp.reshape(B * T, D)
    q, k, kb, v, vb, gate, lf = _fox_proj(x2, gain, w_main, w_tail, b_tail, qg, kg, _row_tile(B * T, 512), 512)
    lf_p = lf[:, :H].reshape(B, T, H)
    cum_t = _time_cumsum(jnp.swapaxes(lf_p, 1, 2))
    og = _fox_attn(q, kb, vb, cum_t.reshape(B * H, 1, T), gate, B, T, H, dh, min(512, T))
    yp_new = _oproj(og, w_out, x2, _row_tile(B * T, 512), 512).reshape(B, T, D)

    xs2 = ys.reshape(Bs * Ts, D)
    qs, ks, _, vs, _, gate_s, lfs = _fox_proj(xs2, gain, w_main, w_tail, b_tail, qg, kg, Bs * Ts, 512)
    n_pool, page = cache_k.shape[0], cache_k.shape[1]
    lf_s = lfs[:, :H].reshape(Bs, Ts, H)
    pad_rows = lambda a: jnp.pad(a.reshape(Bs, Ts, D), ((0, 0), (0, page - Ts), (0, 0)))
    lf_new_t = jnp.pad(jnp.swapaxes(lf_s, 1, 2), ((0, 0), (0, 0), (0, page - Ts)))
    ogs = _fox_decode(page_table, qs.astype(F32).reshape(Bs, Ts, D),
                      cache_k.reshape(n_pool, page, D), cache_v.reshape(n_pool, page, D),
                      jnp.swapaxes(cache_logf, 1, 2), pad_rows(ks), pad_rows(vs), lf_new_t,
                      gate_s.reshape(Bs, Ts, D), H, dh)
    ys_new = _oproj(ogs.reshape(Bs * Ts, D).astype(BF16), w_out, xs2, Bs * Ts, 512).reshape(Bs, Ts, D)

    new = (k.reshape(B, T, H, dh), v.reshape(B, T, H, dh), lf_p,
           ks.reshape(Bs, Ts, H, dh), vs.reshape(Bs, Ts, H, dh), lf_s)
    return yp_new, ys_new, new


def _gdn_gates_kernel(x_ref, al_ref, dt_ref, col_ref, row_ref, *, t_valid, nvh):
    x = x_ref[...]
    tb = x.shape[0]
    valid = pl.program_id(1) * tb + _iota(x.shape, 0) < t_valid
    beta = jnp.where(valid, _sigmoid(x), 0.0)
    g = jnp.where(valid, -jnp.exp(al_ref[...]) * _softplus(x + dt_ref[...]), 0.0)
    out = jnp.where(_iota(x.shape, 1) < nvh, beta, _chunk_row_cumsum(g, CHUNK))
    col_ref[...] = out
    row_ref[...] = out.T


def _gdn_gates(zt, a_log_l, dt_l, B, T, t_valid, nvh):
    tb = LANES
    nT = T // tb
    kern = functools.partial(_gdn_gates_kernel, t_valid=t_valid, nvh=nvh)
    return pl.pallas_call(
        kern,
        grid=(B, nT),
        in_specs=[pl.BlockSpec((tb, LANES), lambda b, t: (b * nT + t, 0)),
                  pl.BlockSpec((1, LANES), lambda b, t: (0, 0)),
                  pl.BlockSpec((1, LANES), lambda b, t: (0, 0))],
        out_specs=[pl.BlockSpec((tb, LANES), lambda b, t: (b * nT + t, 0)),
                   pl.BlockSpec((None, LANES, tb), lambda b, t: (b, 0, t))],
        out_shape=[jax.ShapeDtypeStruct((B * T, LANES), F32), jax.ShapeDtypeStruct((B, LANES, T), F32)],
        compiler_params=_params("parallel", "parallel"),
        name="gdn_gates",
    )(zt, a_log_l, dt_l)


def _gdn_conv_kernel(z_ref, w_ref, init_ref, o_ref, ext_ref, *, n_qk_tiles, dh):
    c = pl.program_id(1)
    tb = z_ref.shape[0]

    @pl.when(pl.program_id(2) == 0)
    def _():
        ext_ref[0:8, :] = init_ref[...]

    ext_ref[8:8 + tb, :] = z_ref[...]
    w = w_ref[...]
    y = (w[0:1, :] * ext_ref[5:5 + tb, :] + w[1:2, :] * ext_ref[6:6 + tb, :]
         + w[2:3, :] * ext_ref[7:7 + tb, :] + w[3:4, :] * ext_ref[8:8 + tb, :])
    y = y * _sigmoid(y)
    ext_ref[0:8, :] = ext_ref[tb:tb + 8, :]

    def l2(scale):
        for h in range(y.shape[1] // dh):
            sl = slice(h * dh, (h + 1) * dh)
            yh = y[:, sl]
            o_ref[:, sl] = yh * (lax.rsqrt(jnp.sum(yh * yh, axis=-1, keepdims=True) + EPS) * scale)

    @pl.when(c < n_qk_tiles)
    def _():
        l2(dh ** -0.5)

    @pl.when(jnp.logical_and(c >= n_qk_tiles, c < 2 * n_qk_tiles))
    def _():
        l2(1.0)

    @pl.when(c >= 2 * n_qk_tiles)
    def _():
        o_ref[...] = y


def _gdn_conv(z, conv_w, init, B, T, conv_dim, key_dim, dh, tb, tc):
    nT = T // tb
    kern = functools.partial(_gdn_conv_kernel, n_qk_tiles=key_dim // tc, dh=dh)
    return pl.pallas_call(
        kern,
        grid=(B, conv_dim // tc, nT),
        in_specs=[pl.BlockSpec((tb, tc), lambda b, c, t: (b * nT + t, c)),
                  pl.BlockSpec((conv_w.shape[0], tc), lambda b, c, t: (0, c)),
                  pl.BlockSpec((None, 8, tc), lambda b, c, t: (b, 0, c))],
        out_specs=pl.BlockSpec((tb, tc), lambda b, c, t: (b * nT + t, c)),
        out_shape=jax.ShapeDtypeStruct((B * T, conv_dim), F32),
        scratch_shapes=[pltpu.VMEM((tb + 8, tc), F32)],
        compiler_params=_params("parallel", "parallel", "arbitrary"),
        name="gdn_conv",
    )(z, conv_w, init)


def _tril_levels(n, chunk):
    i = lax.broadcasted_iota(jnp.int32, (n, n), 0)
    j = lax.broadcasted_iota(jnp.int32, (n, n), 1)
    lvl = jnp.full((n, n), -1, jnp.int32)
    s = 1
    e = 0
    while s < chunk:
        hit = ((i // (2 * s)) == (j // (2 * s))) & ((i // s) % 2 == 1) & ((j // s) % 2 == 0)
        lvl = jnp.where(hit, e, lvl)
        s *= 2
        e += 1
    return lvl


def _gdn_chunk_kernel(q_ref, k_ref, v_ref, zg_ref, col_ref, grow_ref, ng_ref, lvl_ref, s0_ref,
                      og_ref, sout_ref, s_ref, *, nvh, dh):
    hq = pl.program_id(1)
    t = pl.program_id(2)
    L = CHUNK
    tb = q_ref.shape[0]
    nchunk = tb // L
    rep = s_ref.shape[0]

    @pl.when(t == 0)
    def _():
        s_ref[...] = s0_ref[...]

    q = q_ref[...]
    k = k_ref[...]
    colb = col_ref[...]
    kk = _mm_nt(k, k)
    qk = _mm_nt(q, k)
    ri = _iota((tb, tb), 0)
    ci = _iota((tb, tb), 1)
    same = (ri // L) == (ci // L)
    causal = jnp.logical_and(same, ci <= ri)
    strict = jnp.logical_and(same, ci < ri)

    g_cols, qkd, m_blocks, rhs = [], [], [], []
    for r in range(rep):
        beta_c = _pick_lane(colb, rep * hq + r)
        g_c = _pick_lane(colb, nvh + rep * hq + r)
        g_r = grow_ref[r:r + 1, :]
        decay = jnp.where(causal, jnp.exp(jnp.where(causal, g_c - g_r, 0.0)), 0.0)
        m_blocks.append(jnp.where(strict, beta_c * kk * decay, 0.0))
        qkd.append(qk * decay)
        g_cols.append(g_c)
        kb = k * beta_c
        rhs.append(jnp.concatenate([v_ref[:, r * dh:(r + 1) * dh] * beta_c, kb * jnp.exp(g_c)], axis=1))

    n = rep * tb
    zero = jnp.zeros((tb, tb), F32)
    m_full = jnp.concatenate(
        [jnp.concatenate([m_blocks[r] if rr == r else zero for rr in range(rep)], axis=1) for r in range(rep)],
        axis=0)
    lvl = lvl_ref[...]
    eye = (_iota((n, n), 0) == _iota((n, n), 1)).astype(F32)
    tinv = eye - jnp.where(lvl == 0, m_full, 0.0)
    s, e = 2, 1
    while s < L:
        tinv = tinv - _mm3(_mm3(tinv, jnp.where(lvl == e, m_full, 0.0)), tinv)
        s *= 2
        e += 1
    sol = _mm3(tinv, jnp.concatenate(rhs, axis=0))

    for c in range(nchunk):
        sl = slice(c * L, (c + 1) * L)
        for r in range(rep):
            g_c = g_cols[r][sl]
            u = sol[r * tb + c * L:r * tb + (c + 1) * L, :dh]
            w = sol[r * tb + c * L:r * tb + (c + 1) * L, dh:]
            S = s_ref[r]
            v_new = u - _mm(w, S)
            o = _mm(q[sl] * jnp.exp(g_c), S) + _mm(qkd[r][sl, sl], v_new)
            g_last = g_c[L - 1:L, :]
            k_dec = k[sl] * jnp.exp(g_last - g_c)
            s_ref[r] = S * jnp.exp(g_last) + _mm_tn(k_dec, v_new)
            zg = zg_ref[sl, r * dh:(r + 1) * dh]
            og_ref[sl, r * dh:(r + 1) * dh] = (_rms(o, ng_ref[...]) * (zg * _sigmoid(zg))).astype(BF16)

    @pl.when(t == pl.num_programs(2) - 1)
    def _():
        sout_ref[...] = s_ref[...]


def _gdn_chunk(qkv, z, gcol, grow, norm_g, s0, B, T, nqk, nvh, dh):
    tb = LANES
    nT = T // tb
    rep = nvh // nqk
    key_dim = nqk * dh
    grow4 = grow.reshape(B, LANES // rep, rep, T)
    kern = functools.partial(_gdn_chunk_kernel, nvh=nvh, dh=dh)
    row = lambda off: (lambda b, h, t: (b * nT + t, off + h))
    return pl.pallas_call(
        kern,
        grid=(B, nqk, nT),
        in_specs=[pl.BlockSpec((tb, dh), row(0)),
                  pl.BlockSpec((tb, dh), row(nqk)),
                  pl.BlockSpec((tb, rep * dh), row(2 * key_dim // (rep * dh))),
                  pl.BlockSpec((tb, rep * dh), row((2 * key_dim + nvh * dh) // (rep * dh))),
                  pl.BlockSpec((tb, LANES), lambda b, h, t: (b * nT + t, 0)),
                  pl.BlockSpec((None, None, rep, tb), lambda b, h, t: (b, nvh // rep + h, 0, t)),
                  pl.BlockSpec((1, dh), lambda b, h, t: (0, 0)),
                  pl.BlockSpec((rep * tb, rep * tb), lambda b, h, t: (0, 0)),
                  pl.BlockSpec((None, rep, dh, dh), lambda b, h, t: (b, h, 0, 0))],
        out_specs=[pl.BlockSpec((tb, rep * dh), lambda b, h, t: (b * nT + t, h)),
                   pl.BlockSpec((None, rep, dh, dh), lambda b, h, t: (b, h, 0, 0))],
        out_shape=[jax.ShapeDtypeStruct((B * T, nvh * dh), BF16),
                   jax.ShapeDtypeStruct((B, nvh, dh, dh), F32)],
        scratch_shapes=[pltpu.VMEM((rep, dh, dh), F32)],
        compiler_params=_params("parallel", "parallel", "arbitrary"),
        name="gdn_chunk",
    )(qkv, qkv, qkv, z, gcol, grow4, norm_g, _tril_levels(rep * tb, CHUNK), s0)


def _pad_time(a, B, T, Tp):
    if Tp == T:
        return a
    return jnp.pad(a.reshape(B, T, -1), ((0, 0), (0, Tp - T), (0, 0))).reshape(B * Tp, -1)


def _gdn_mix(y, gain, conv_state, s0, w_in, conv_w, a_log, dt_bias, norm_g, w_out, nqk, tm):
    B, T, D = y.shape
    nvh = a_log.shape[0]
    dh = norm_g.shape[0]
    key_dim = nqk * dh
    conv_dim = 2 * key_dim + nvh * dh
    n_main = conv_dim + nvh * dh
    x2 = y.reshape(B * T, D)
    z, zt = _proj(x2, gain.reshape(1, D), w_in[:, :n_main], _pad_cols(w_in[:, n_main:], LANES), tm, 512)
    Tp = -(-T // LANES) * LANES
    zp, ztp = _pad_time(z, B, T, Tp), _pad_time(zt, B, T, Tp)
    lane_par = lambda p: jnp.pad(p.reshape(1, nvh), ((0, 0), (nvh, LANES - 2 * nvh)))
    gcol, grow = _gdn_gates(ztp, lane_par(a_log), lane_par(dt_bias), B, Tp, T, nvh)
    init = jnp.pad(conv_state, ((0, 0), (8 - conv_state.shape[1], 0), (0, 0)))
    qkv = _gdn_conv(zp, conv_w, init, B, Tp, conv_dim, key_dim, dh, min(256, Tp), 512)
    og, s_new = _gdn_chunk(qkv, zp, gcol, grow, norm_g.reshape(1, dh), s0, B, Tp, nqk, nvh, dh)
    og = og.reshape(B, Tp, nvh * dh)[:, :T].reshape(B * T, nvh * dh)
    y_new = _oproj(og, w_out, x2, tm, 512).reshape(B, T, D)
    full = jnp.concatenate([conv_state, z[:, :conv_dim].reshape(B, T, conv_dim)], axis=1)
    return y_new, full[:, T:], s_new


def _mlstm_gates_kernel(xi_ref, xf_ref, bi_ref, bf_ref, col_ref, row_ref, *, t_valid):
    xi = xi_ref[...]
    tb = xi.shape[0]
    valid = pl.program_id(1) * tb + _iota(xi.shape, 0) < t_valid
    ig = GATE_SOFTCAP * jnp.tanh((xi + bi_ref[...]) / GATE_SOFTCAP)
    lf = _log_sigmoid(GATE_SOFTCAP * jnp.tanh((xf_ref[...] + bf_ref[...]) / GATE_SOFTCAP))
    ig = jnp.where(valid, ig, NEG)
    a = _chunk_row_cumsum(jnp.where(valid, lf, 0.0), CHUNK)
    ia = ig - a
    col_ref[:, :LANES] = a
    col_ref[:, LANES:] = ia
    row_ref[...] = ia.T


def _mlstm_gates(zt, b_i_l, b_f_l, B, T, t_valid):
    tb = LANES
    nT = T // tb
    kern = functools.partial(_mlstm_gates_kernel, t_valid=t_valid)
    return pl.pallas_call(
        kern,
        grid=(B, nT),
        in_specs=[pl.BlockSpec((tb, LANES), lambda b, t: (b * nT + t, 0)),
                  pl.BlockSpec((tb, LANES), lambda b, t: (b * nT + t, 1)),
                  pl.BlockSpec((1, LANES), lambda b, t: (0, 0)),
                  pl.BlockSpec((1, LANES), lambda b, t: (0, 0))],
        out_specs=[pl.BlockSpec((tb, 2 * LANES), lambda b, t: (b * nT + t, 0)),
                   pl.BlockSpec((None, LANES, tb), lambda b, t: (b, 0, t))],
        out_shape=[jax.ShapeDtypeStruct((B * T, 2 * LANES), F32), jax.ShapeDtypeStruct((B, LANES, T), F32)],
        compiler_params=_params("parallel", "parallel"),
        name="mlstm_gates",
    )(zt, zt, b_i_l, b_f_l)


def _mlstm_chunk_kernel(q_ref, k_ref, v_ref, og_ref, col_ref, row_ref, ng_ref, c0_ref, n0_ref, m0_ref,
                        out_ref, cout_ref, nout_ref, mout_ref, c_ref, n_ref, m_ref, *, dqk):
    h = pl.program_id(1)
    t = pl.program_id(2)
    L = CHUNK
    tb = q_ref.shape[0]

    @pl.when(t == 0)
    def _():
        c_ref[...] = c0_ref[...]
        n_ref[...] = n0_ref[...]
        m_ref[...] = m0_ref[...]

    causal = _iota((L, L), 1) <= _iota((L, L), 0)
    for c in range(tb // L):
        sl = slice(c * L, (c + 1) * L)
        q = q_ref[sl, :]
        k = k_ref[sl, :] * (dqk ** -0.5)
        v = v_ref[sl, :]
        a_c = _pick_lane(col_ref[sl, :LANES], h)
        ia_c = _pick_lane(col_ref[sl, LANES:], h)
        ia_r = row_ref[:, sl]
        dmat = jnp.where(causal, a_c + ia_r, NEG)
        dmax = jnp.max(dmat, axis=1, keepdims=True)
        m = m_ref[...]
        inter = a_c + m
        m_row = jnp.maximum(inter, dmax)
        w_inter = jnp.exp(inter - m_row)
        smat = _mm_nt(q, k) * jnp.exp(dmat - m_row)
        C = c_ref[...]
        nvec = n_ref[...]
        num = w_inter * _mm(q, C) + _mm(smat, v)
        den = w_inter * jnp.sum(q * nvec, axis=1, keepdims=True) + jnp.sum(smat, axis=1, keepdims=True)
        hh = num / jnp.maximum(jnp.abs(den), jnp.exp(-m_row))
        m_new = m_row[L - 1:L, :]
        a_last = a_c[L - 1:L, :]
        scale = jnp.exp(a_last + m - m_new)
        kw = k * jnp.exp(a_last + ia_c - m_new)
        c_ref[...] = scale * C + _mm_tn(kw, v)
        n_ref[...] = scale * nvec + jnp.sum(kw, axis=0, keepdims=True)
        m_ref[...] = m_new
        out_ref[sl, :] = (_rms(hh, ng_ref[...]) * _sigmoid(og_ref[sl, :])).astype(BF16)

    @pl.when(t == pl.num_programs(2) - 1)
    def _():
        cout_ref[...] = c_ref[...]
        nout_ref[...] = n_ref[...]
        mout_ref[...] = m_ref[...]


def _mlstm_chunk(z, gcol, grow, norm_g, c0, n0, m0, B, T, nh, dqk, dv):
    tb = LANES
    nT = T // tb
    kern = functools.partial(_mlstm_chunk_kernel, dqk=dqk)
    row = lambda off: (lambda b, h, t: (b * nT + t, off + h))
    st = lambda b, h, t: (b, h, 0, 0)
    return pl.pallas_call(
        kern,
        grid=(B, nh, nT),
        in_specs=[pl.BlockSpec((tb, dqk), row(0)),
                  pl.BlockSpec((tb, dqk), row(nh)),
                  pl.BlockSpec((tb, dv), row(2 * nh * dqk // dv)),
                  pl.BlockSpec((tb, dv), row(2 * nh * dqk // dv + nh)),
                  pl.BlockSpec((tb, 2 * LANES), lambda b, h, t: (b * nT + t, 0)),
                  pl.BlockSpec((None, None, 1, tb), lambda b, h, t: (b, h, 0, t)),
                  pl.BlockSpec((None, 1, dv), lambda b, h, t: (h, 0, 0)),
                  pl.BlockSpec((None, None, dqk, dv), st),
                  pl.BlockSpec((None, None, 1, dqk), st),
                  pl.BlockSpec((None, None, 1, 1), st)],
        out_specs=[pl.BlockSpec((tb, dv), lambda b, h, t: (b * nT + t, h)),
                   pl.BlockSpec((None, None, dqk, dv), st),
                   pl.BlockSpec((None, None, 1, dqk), st),
                   pl.BlockSpec((None, None, 1, 1), st)],
        out_shape=[jax.ShapeDtypeStruct((B * T, nh * dv), BF16),
                   jax.ShapeDtypeStruct((B, nh, dqk, dv), F32),
                   jax.ShapeDtypeStruct((B, nh, 1, dqk), F32),
                   jax.ShapeDtypeStruct((B, nh, 1, 1), F32)],
        scratch_shapes=[pltpu.VMEM((dqk, dv), F32), pltpu.VMEM((1, dqk), F32), pltpu.VMEM((1, 1), F32)],
        compiler_params=_params("parallel", "parallel", "arbitrary"),
        name="mlstm_chunk",
    )(z, z, z, z, gcol, grow.reshape(B, LANES, 1, T), norm_g, c0, n0, m0)


def _mlstm_mix(y, gain, c0, n0, m0, w_in, b_i, b_f, norm_g, w_out, tm):
    B, T, D = y.shape
    nh = b_i.shape[0]
    dqk = c0.shape[2]
    dv = c0.shape[3]
    n_main = 2 * nh * dqk + nh * dv + D
    x2 = y.reshape(B * T, D)
    w_gate = w_in[:, n_main:]
    w_tail = jnp.concatenate([_pad_cols(w_gate[:, :nh], LANES), _pad_cols(w_gate[:, nh:], LANES)], axis=1)
    z, zt = _proj(x2, gain.reshape(1, D), w_in[:, :n_main], w_tail, tm, 512)
    Tp = -(-T // LANES) * LANES
    zp, ztp = _pad_time(z, B, T, Tp), _pad_time(zt, B, T, Tp)
    gcol, grow = _mlstm_gates(ztp, _pad_cols(b_i.reshape(1, nh), LANES), _pad_cols(b_f.reshape(1, nh), LANES),
                              B, Tp, T)
    og, c_new, n_new, m_new = _mlstm_chunk(zp, gcol, grow, norm_g.reshape(nh, 1, dv), c0,
                                           n0.reshape(B, nh, 1, dqk), m0.reshape(B, nh, 1, 1), B, Tp, nh, dqk, dv)
    og = og.reshape(B, Tp, nh * dv)[:, :T].reshape(B * T, nh * dv)
    y_new = _oproj(og, w_out, x2, tm, 512).reshape(B, T, D)
    return y_new, c_new, n_new.reshape(B, nh, dqk), m_new.reshape(B, nh)


def _mlp_layer(y, gain, w_up, w_down, tm):
    B, T, D = y.shape
    return _mlp(y.reshape(B * T, D), gain.reshape(1, D), w_up, w_down, tm, 512).reshape(B, T, D)


def kernel(x_prompt, x_sample, cache_k_l0, cache_v_l0, cache_logf_l0, state_conv_l1, state_ssm_l1, state_C_l2, state_n_l2, state_m_l2, cache_k_l3, cache_v_l3, cache_logf_l3, page_table, norm_mix, norm_mlp, w_up, w_down, fox_w_in, fox_b_f, fox_q_norm, fox_k_norm, fox_w_out, gdn_w_in, gdn_conv_w, gdn_A_log, gdn_dt_bias, gdn_norm, gdn_w_out, mlstm_w_in, mlstm_b_i, mlstm_b_f, mlstm_norm, mlstm_w_out):
    yp, ys = x_prompt, x_sample
    B, T, D = yp.shape
    Bs, Ts, _ = ys.shape
    tm_p = _row_tile(B * T, 512)
    tm_s = Bs * Ts
    nqk = D // gdn_norm.shape[1]

    def mlp(i, yp, ys):
        return (_mlp_layer(yp, norm_mlp[i], w_up[i], w_down[i], tm_p),
                _mlp_layer(ys, norm_mlp[i], w_up[i], w_down[i], tm_s))

    yp, ys, fox0 = _fox_layer(yp, ys, cache_k_l0, cache_v_l0, cache_logf_l0, page_table, norm_mix[0],
                              fox_w_in[0], fox_b_f[0], fox_q_norm[0], fox_k_norm[0], fox_w_out[0])
    yp, ys = mlp(0, yp, ys)

    gw = (gdn_w_in[0], gdn_conv_w[0], gdn_A_log[0], gdn_dt_bias[0], gdn_norm[0], gdn_w_out[0])
    nvh, dh = state_ssm_l1.shape[1], state_ssm_l1.shape[2]
    yp, conv_p, ssm_p = _gdn_mix(yp, norm_mix[1], jnp.zeros((B,) + state_conv_l1.shape[1:], F32),
                                 jnp.zeros((B, nvh, dh, dh), F32), *gw, nqk, tm_p)
    ys, conv_s, ssm_s = _gdn_mix(ys, norm_mix[1], state_conv_l1, state_ssm_l1, *gw, nqk, tm_s)
    yp, ys = mlp(1, yp, ys)

    mw = (mlstm_w_in[0], mlstm_b_i[0], mlstm_b_f[0], mlstm_norm[0], mlstm_w_out[0])
    zc = lambda a: jnp.zeros((B,) + a.shape[1:], F32)
    yp, c_p, n_p, m_p = _mlstm_mix(yp, norm_mix[2], zc(state_C_l2), zc(state_n_l2), zc(state_m_l2), *mw, tm_p)
    ys, c_s, n_s, m_s = _mlstm_mix(ys, norm_mix[2], state_C_l2, state_n_l2, state_m_l2, *mw, tm_s)
    yp, ys = mlp(2, yp, ys)

    yp, ys, fox3 = _fox_layer(yp, ys, cache_k_l3, cache_v_l3, cache_logf_l3, page_table, norm_mix[3],
                              fox_w_in[1], fox_b_f[1], fox_q_norm[1], fox_k_norm[1], fox_w_out[1])
    yp, ys = mlp(3, yp, ys)

    return (yp, ys) + fox0 + (conv_p, ssm_p, conv_s, ssm_s, c_p, n_p, m_p, c_s, n_s, m_s) + fox3
```

```python
import functools

import jax
import jax.numpy as jnp
from jax import lax
from jax.experimental import pallas as pl
from jax.experimental.pallas import tpu as pltpu

F32 = jnp.float32
BF16 = jnp.bfloat16
EPS = 1e-6
NEG = -1e30
LANES = 128
CHUNK = 64
VMEM_LIMIT = 56 * 1024 * 1024
ROW_TILE = 1024
COL_TILE = 512
ATTN_BLOCK = 512
ATTN_ROWS = 256
GDN_HEADS_PER_STEP = 4
GATE_SOFTCAP = 15.0
_NT = (((1,), (1,)), ((), ()))
_TN = (((0,), (0,)), ((), ()))


def _params(*sem):
    return pltpu.CompilerParams(dimension_semantics=sem, vmem_limit_bytes=VMEM_LIMIT)


def _mm(a, b):
    return jnp.dot(a.astype(BF16), b.astype(BF16), preferred_element_type=F32)


def _mm_nt(a, b):
    return lax.dot_general(a.astype(BF16), b.astype(BF16), _NT, preferred_element_type=F32)


def _mm_tn(a, b):
    return lax.dot_general(a.astype(BF16), b.astype(BF16), _TN, preferred_element_type=F32)


def _sigmoid(x):
    return 1.0 / (1.0 + jnp.exp(-x))


def _softplus(x):
    return jnp.maximum(x, 0.0) + jnp.log(1.0 + jnp.exp(-jnp.abs(x)))


def _log_sigmoid(x):
    return -_softplus(-x)


def _rms(x, gain):
    return x * lax.rsqrt(jnp.mean(x * x, axis=-1, keepdims=True) + EPS) * gain


def _iota(shape, axis):
    return lax.broadcasted_iota(jnp.int32, shape, axis)


def _lane_cumsum(x):
    n = x.shape[-1]
    lane = _iota(x.shape, x.ndim - 1)
    s = 1
    while s < n:
        x = x + jnp.where(lane >= s, pltpu.roll(x, s, x.ndim - 1), 0.0)
        s *= 2
    return x


def _chunk_row_cumsum(x, chunk):
    pos = _iota(x.shape, 0) % chunk
    s = 1
    while s < chunk:
        x = x + jnp.where(pos >= s, pltpu.roll(x, s, 0), 0.0)
        s *= 2
    return x


def _pick_lane(block, lane_idx):
    return jnp.sum(jnp.where(_iota(block.shape, 1) == lane_idx, block, 0.0), axis=1, keepdims=True)


def _proj_kernel(x_ref, g_ref, w_ref, wt_ref, z_ref, zt_ref, xn_ref):
    @pl.when(pl.program_id(1) == 0)
    def _():
        xn = _rms(x_ref[...], g_ref[...]).astype(BF16)
        xn_ref[...] = xn
        zt_ref[...] = jnp.dot(xn, wt_ref[...].astype(BF16), preferred_element_type=F32)

    z_ref[...] = jnp.dot(xn_ref[...], w_ref[...].astype(BF16), preferred_element_type=F32)


def _proj(x, gain, w_in, N, w_tail, tm, tn):
    M, D = x.shape
    NT_ = w_tail.shape[1]
    return pl.pallas_call(
        _proj_kernel,
        grid=(M // tm, N // tn),
        in_specs=[pl.BlockSpec((tm, D), lambda i, j: (i, 0)),
                  pl.BlockSpec((1, D), lambda i, j: (0, 0)),
                  pl.BlockSpec((D, tn), lambda i, j: (0, j)),
                  pl.BlockSpec((D, NT_), lambda i, j: (0, 0))],
        out_specs=[pl.BlockSpec((tm, tn), lambda i, j: (i, j)),
                   pl.BlockSpec((tm, NT_), lambda i, j: (i, 0))],
        out_shape=[jax.ShapeDtypeStruct((M, N), F32), jax.ShapeDtypeStruct((M, NT_), F32)],
        scratch_shapes=[pltpu.VMEM((tm, D), BF16)],
        compiler_params=_params("parallel", "arbitrary"),
        name="proj",
    )(x, gain, w_in, w_tail)


def _oproj_kernel(a_ref, w_ref, r_ref, o_ref):
    o_ref[...] = r_ref[...] + jnp.dot(a_ref[...], w_ref[...].astype(BF16), preferred_element_type=F32)


def _oproj(a, w, resid, tm, tn):
    M, K = a.shape
    N = w.shape[1]
    return pl.pallas_call(
        _oproj_kernel,
        grid=(M // tm, N // tn),
        in_specs=[pl.BlockSpec((tm, K), lambda i, j: (i, 0)),
                  pl.BlockSpec((K, tn), lambda i, j: (0, j)),
                  pl.BlockSpec((tm, tn), lambda i, j: (i, j))],
        out_specs=pl.BlockSpec((tm, tn), lambda i, j: (i, j)),
        out_shape=jax.ShapeDtypeStruct((M, N), F32),
        compiler_params=_params("parallel", "arbitrary"),
        name="oproj",
    )(a, w, resid)


def _mlp_kernel(x_ref, g_ref, wu_ref, wd_ref, o_ref, xn_ref):
    @pl.when(pl.program_id(1) == 0)
    def _():
        x = x_ref[...]
        xn_ref[...] = _rms(x, g_ref[...]).astype(BF16)
        o_ref[...] = x

    h = jnp.dot(xn_ref[...], wu_ref[...].astype(BF16), preferred_element_type=F32)
    h = jnp.maximum(h, 0.0)
    o_ref[...] += jnp.dot((h * h).astype(BF16), wd_ref[...].astype(BF16), preferred_element_type=F32)


def _mlp(x, gain, w_up, w_down, tm, tf):
    M, D = x.shape
    Fdim = w_up.shape[1]
    return pl.pallas_call(
        _mlp_kernel,
        grid=(M // tm, Fdim // tf),
        in_specs=[pl.BlockSpec((tm, D), lambda i, j: (i, 0), pipeline_mode=pl.Buffered(1)),
                  pl.BlockSpec((1, D), lambda i, j: (0, 0)),
                  pl.BlockSpec((D, tf), lambda i, j: (0, j)),
                  pl.BlockSpec((tf, D), lambda i, j: (j, 0))],
        out_specs=pl.BlockSpec((tm, D), lambda i, j: (i, 0), pipeline_mode=pl.Buffered(1)),
        out_shape=jax.ShapeDtypeStruct((M, D), F32),
        scratch_shapes=[pltpu.VMEM((tm, D), BF16)],
        compiler_params=_params("parallel", "arbitrary"),
        name="mlp",
    )(x, gain, w_up, w_down)


def _fox_proj_kernel(x_ref, g_ref, w_ref, wt_ref, bt_ref, qg_ref, kg_ref,
                     q_ref, k_ref, kb_ref, v_ref, vb_ref, gate_ref, lf_ref, xn_ref, *, nseg, dh):
    j = pl.program_id(1)

    @pl.when(j == 0)
    def _():
        xn = _rms(x_ref[...], g_ref[...]).astype(BF16)
        xn_ref[...] = xn
        zt = jnp.dot(xn, wt_ref[...].astype(BF16), preferred_element_type=F32) + bt_ref[...]
        lf_ref[...] = _log_sigmoid(zt)

    acc = jnp.dot(xn_ref[...], w_ref[...].astype(BF16), preferred_element_type=F32)
    heads = acc.shape[1] // dh
    seg = j // nseg

    @pl.when(seg == 0)
    def _():
        for h in range(heads):
            sl = slice(h * dh, (h + 1) * dh)
            q_ref[:, sl] = _rms(acc[:, sl], qg_ref[...]).astype(BF16)

    @pl.when(seg == 1)
    def _():
        for h in range(heads):
            sl = slice(h * dh, (h + 1) * dh)
            kn = _rms(acc[:, sl], kg_ref[...])
            k_ref[:, sl] = kn
            kb_ref[:, sl] = kn.astype(BF16)

    @pl.when(seg == 2)
    def _():
        v_ref[...] = acc
        vb_ref[...] = acc.astype(BF16)

    @pl.when(seg == 3)
    def _():
        gate_ref[...] = acc


def _fox_proj(x, gain, w_main, w_tail, b_tail, q_gain, k_gain, tm, tn):
    M, D = x.shape
    dh = q_gain.shape[1]
    nseg = D // tn

    def seg_map(s):
        return lambda i, j: (i, jnp.clip(j - s * nseg, 0, nseg - 1))

    row = lambda i, j: (i, 0)
    const = lambda i, j: (0, 0)
    kern = functools.partial(_fox_proj_kernel, nseg=nseg, dh=dh)
    return pl.pallas_call(
        kern,
        grid=(M // tm, 4 * nseg),
        in_specs=[pl.BlockSpec((tm, D), row),
                  pl.BlockSpec((1, D), const),
                  pl.BlockSpec((D, tn), lambda i, j: (0, j)),
                  pl.BlockSpec((D, LANES), const),
                  pl.BlockSpec((1, LANES), const),
                  pl.BlockSpec((1, dh), const),
                  pl.BlockSpec((1, dh), const)],
        out_specs=[pl.BlockSpec((tm, tn), seg_map(0)),
                   pl.BlockSpec((tm, tn), seg_map(1)),
                   pl.BlockSpec((tm, tn), seg_map(1)),
                   pl.BlockSpec((tm, tn), seg_map(2)),
                   pl.BlockSpec((tm, tn), seg_map(2)),
                   pl.BlockSpec((tm, tn), seg_map(3)),
                   pl.BlockSpec((tm, LANES), row)],
        out_shape=[jax.ShapeDtypeStruct((M, D), BF16),
                   jax.ShapeDtypeStruct((M, D), F32),
                   jax.ShapeDtypeStruct((M, D), BF16),
                   jax.ShapeDtypeStruct((M, D), F32),
                   jax.ShapeDtypeStruct((M, D), BF16),
                   jax.ShapeDtypeStruct((M, D), F32),
                   jax.ShapeDtypeStruct((M, LANES), F32)],
        scratch_shapes=[pltpu.VMEM((tm, D), BF16)],
        compiler_params=_params("parallel", "arbitrary"),
        name="fox_proj",
    )(x, gain, w_main, w_tail, b_tail, q_gain, k_gain)


def _cumsum_kernel(x_ref, o_ref):
    o_ref[...] = _lane_cumsum(x_ref[...])


def _time_cumsum(x):
    B, H, T = x.shape
    return pl.pallas_call(
        _cumsum_kernel,
        grid=(B,),
        in_specs=[pl.BlockSpec((None, H, T), lambda b: (b, 0, 0))],
        out_specs=pl.BlockSpec((None, H, T), lambda b: (b, 0, 0)),
        out_shape=jax.ShapeDtypeStruct((B, H, T), F32),
        compiler_params=_params("parallel"),
        name="time_cumsum",
    )(x)


def _fox_attn_kernel(qi_ref, ki_ref, q_ref, k_ref, v_ref, ck_ref, g_ref, o_ref, m_ref, acc_ref, *, scale, rows):
    qi = qi_ref[pl.program_id(2)]
    ki = ki_ref[pl.program_id(2)]
    blk, dh = q_ref.shape

    @pl.when(ki == 0)
    def _():
        m_ref[...] = jnp.full(m_ref.shape, NEG, F32)
        acc_ref[...] = jnp.zeros(acc_ref.shape, F32)

    def update(on_diagonal):
        v_ext = jnp.concatenate([v_ref[...], jnp.ones((blk, dh), BF16)], axis=1)
        for r0 in range(0, blk, rows):
            rs = slice(r0, r0 + rows)
            nk = r0 + rows if on_diagonal else blk
            s = lax.dot_general(q_ref[rs, :], k_ref[:nk, :], _NT, preferred_element_type=F32) * scale
            s = s - ck_ref[:, :nk]
            if on_diagonal:
                s = jnp.where(_iota(s.shape, 1) <= _iota(s.shape, 0) + r0, s, NEG)
            m_prev = m_ref[rs, :]
            m_new = jnp.maximum(m_prev, jnp.max(s, axis=1, keepdims=True))
            p = jnp.exp(s - m_new).astype(BF16)
            acc_ref[rs, :] = (jnp.exp(m_prev - m_new) * acc_ref[rs, :]
                              + jnp.dot(p, v_ext[:nk, :], preferred_element_type=F32))
            m_ref[rs, :] = m_new

    @pl.when(ki < qi)
    def _():
        update(False)

    @pl.when(ki == qi)
    def _():
        update(True)
        o = acc_ref[:, :dh] / acc_ref[:, dh:] * _sigmoid(g_ref[...])
        o_ref[...] = o.astype(BF16)


def _fox_attn(q, kb, vb, cum_t, gate, B, T, H, dh, blk):
    M, D = q.shape
    nb = T // blk
    pairs = [(qi, ki) for qi in range(nb) for ki in range(qi + 1)]
    qi_tab = jnp.asarray([p[0] for p in pairs], jnp.int32)
    ki_tab = jnp.asarray([p[1] for p in pairs], jnp.int32)
    kern = functools.partial(_fox_attn_kernel, scale=dh ** -0.5, rows=min(ATTN_ROWS, blk))
    qmap = lambda b, h, s, qt, kt: (b * nb + qt[s], h)
    kmap = lambda b, h, s, qt, kt: (b * nb + kt[s], h)
    grid_spec = pltpu.PrefetchScalarGridSpec(
        num_scalar_prefetch=2,
        grid=(B, H, len(pairs)),
        in_specs=[pl.BlockSpec((blk, dh), qmap),
                  pl.BlockSpec((blk, dh), kmap),
                  pl.BlockSpec((blk, dh), kmap),
                  pl.BlockSpec((None, 1, blk), lambda b, h, s, qt, kt: (b * H + h, 0, kt[s])),
                  pl.BlockSpec((blk, dh), qmap)],
        out_specs=pl.BlockSpec((blk, dh), qmap),
        scratch_shapes=[pltpu.VMEM((blk, 1), F32), pltpu.VMEM((blk, 2 * dh), F32)])
    return pl.pallas_call(
        kern,
        grid_spec=grid_spec,
        out_shape=jax.ShapeDtypeStruct((M, D), BF16),
        compiler_params=_params("parallel", "parallel", "arbitrary"),
        name="fox_attn",
    )(qi_tab, ki_tab, q, kb, vb, cum_t, gate)


def _fox_decode_kernel(pt_ref, q_ref, kp_ref, vp_ref, lfp_ref, kn_ref, vn_ref, lfn_ref, g_ref, o_ref,
                       qbd_ref, m_ref, l_ref, acc_ref, coff_ref, *, npages, nh, nt, dh, scale):
    p_id = pl.program_id(1)
    R = nt * nh
    D = nh * dh

    def head_mask():
        return (_iota((R, D), 0) % nh) == (_iota((R, D), 1) // dh)

    @pl.when(p_id == 0)
    def _():
        q = q_ref[...]
        rows = jnp.concatenate([jnp.broadcast_to(q[t:t + 1, :], (nh, D)) for t in range(nt)], axis=0)
        qbd_ref[...] = jnp.where(head_mask(), rows, 0.0).astype(BF16)
        m_ref[...] = jnp.full(m_ref.shape, NEG, F32)
        l_ref[...] = jnp.zeros(l_ref.shape, F32)
        acc_ref[...] = jnp.zeros(acc_ref.shape, F32)
        coff_ref[...] = jnp.zeros(coff_ref.shape, F32)

    def step(k, v, lf_t, causal):
        tot = coff_ref[...] + _lane_cumsum(lf_t)
        bias = jnp.concatenate([tot] * nt, axis=0)
        s = lax.dot_general(qbd_ref[...], k.astype(BF16), _NT, preferred_element_type=F32) * scale - bias
        if causal:
            s = jnp.where(_iota(s.shape, 1) <= _iota(s.shape, 0) // nh, s, NEG)
        m_prev = m_ref[...]
        m_new = jnp.maximum(m_prev, jnp.max(s, axis=1, keepdims=True))
        alpha = jnp.exp(m_prev - m_new)
        p = jnp.exp(s - m_new)
        l_ref[...] = alpha * l_ref[...] + jnp.sum(p, axis=1, keepdims=True)
        acc_ref[...] = alpha * acc_ref[...] + jnp.dot(p.astype(BF16), v.astype(BF16), preferred_element_type=F32)
        m_ref[...] = m_new
        coff_ref[...] = tot[:, tot.shape[1] - 1:]

    @pl.when(p_id < npages)
    def _():
        page = lfp_ref.shape[1]
        heads = lambda ref: jnp.concatenate([ref[pl.ds(h, page, stride=nh), :] for h in range(nh)], axis=1)
        step(heads(kp_ref), heads(vp_ref), lfp_ref[...], False)

    @pl.when(p_id == npages)
    def _():
        step(kn_ref[...], vn_ref[...], lfn_ref[...], True)
        o_full = jnp.where(head_mask(), acc_ref[...] / l_ref[...], 0.0)
        o = jnp.concatenate([jnp.sum(o_full[t * nh:(t + 1) * nh, :], axis=0, keepdims=True) for t in range(nt)],
                            axis=0)
        o_ref[...] = o * _sigmoid(g_ref[...])


def _fox_decode(page_table, q, cache_k, cache_v, cache_lf_t, k_new, v_new, lf_new_t, gate, nh, dh):
    B, nt, D = q.shape
    npages = page_table.shape[1]
    page = cache_k.shape[1] // nh
    kern = functools.partial(_fox_decode_kernel, npages=npages, nh=nh, nt=nt, dh=dh, scale=dh ** -0.5)
    pmap = lambda b, p, pt: (pt[b, jnp.minimum(p, npages - 1)], 0, 0)
    bmap = lambda b, p, pt: (b, 0, 0)
    R = nt * nh
    grid_spec = pltpu.PrefetchScalarGridSpec(
        num_scalar_prefetch=1,
        grid=(B, npages + 1),
        in_specs=[pl.BlockSpec((None, nt, D), bmap),
                  pl.BlockSpec((None, page * nh, dh), pmap),
                  pl.BlockSpec((None, page * nh, dh), pmap),
                  pl.BlockSpec((None, nh, page), pmap),
                  pl.BlockSpec((None, page, D), bmap),
                  pl.BlockSpec((None, page, D), bmap),
                  pl.BlockSpec((None, nh, page), bmap),
                  pl.BlockSpec((None, nt, D), bmap)],
        out_specs=pl.BlockSpec((None, nt, D), bmap),
        scratch_shapes=[pltpu.VMEM((R, D), BF16), pltpu.VMEM((R, 1), F32), pltpu.VMEM((R, 1), F32),
                        pltpu.VMEM((R, D), F32), pltpu.VMEM((nh, 1), F32)])
    return pl.pallas_call(
        kern,
        grid_spec=grid_spec,
        out_shape=jax.ShapeDtypeStruct((B, nt, D), F32),
        compiler_params=_params("parallel", "arbitrary"),
        name="fox_decode",
    )(page_table, q, cache_k, cache_v, cache_lf_t, k_new, v_new, lf_new_t, gate)


def _pad_cols(a, n):
    return jnp.pad(a, ((0, 0), (0, n - a.shape[1])))


def _row_tile(m, pref):
    return pref if m % pref == 0 else m


def _fox_layer(yp, ys, cache_k, cache_v, cache_logf, page_table, gain, w_in, b_f, q_gain, k_gain, w_out):
    B, T, D = yp.shape
    Bs, Ts, _ = ys.shape
    H = b_f.shape[0]
    dh = D // H
    w_main = w_in
    w_tail = _pad_cols(w_in[:, 4 * D:], LANES)
    b_tail = _pad_cols(b_f.reshape(1, H), LANES)
    gain = gain.reshape(1, D)
    qg = q_gain.reshape(1, dh)
    kg = k_gain.reshape(1, dh)

    x2 = yp.reshape(B * T, D)
    tm = _row_tile(B * T, ROW_TILE)
    q, k, kb, v, vb, gate, lf = _fox_proj(x2, gain, w_main, w_tail, b_tail, qg, kg, tm, COL_TILE)
    lf_p = lf[:, :H].reshape(B, T, H)
    cum_t = _time_cumsum(jnp.swapaxes(lf_p, 1, 2))
    og = _fox_attn(q, kb, vb, cum_t.reshape(B * H, 1, T), gate, B, T, H, dh, min(ATTN_BLOCK, T))
    yp_new = _oproj(og, w_out, x2, tm, COL_TILE).reshape(B, T, D)

    xs2 = ys.reshape(Bs * Ts, D)
    qs, ks, _, vs, _, gate_s, lfs = _fox_proj(xs2, gain, w_main, w_tail, b_tail, qg, kg, Bs * Ts, COL_TILE)
    n_pool, page = cache_k.shape[0], cache_k.shape[1]
    lf_s = lfs[:, :H].reshape(Bs, Ts, H)
    pad_rows = lambda a: jnp.pad(a.reshape(Bs, Ts, D), ((0, 0), (0, page - Ts), (0, 0)))
    lf_new_t = jnp.pad(jnp.swapaxes(lf_s, 1, 2), ((0, 0), (0, 0), (0, page - Ts)))
    ogs = _fox_decode(page_table, qs.astype(F32).reshape(Bs, Ts, D),
                      cache_k.reshape(n_pool, page * H, dh), cache_v.reshape(n_pool, page * H, dh),
                      jnp.swapaxes(cache_logf, 1, 2), pad_rows(ks), pad_rows(vs), lf_new_t,
                      gate_s.reshape(Bs, Ts, D), H, dh)
    ys_new = _oproj(ogs.reshape(Bs * Ts, D).astype(BF16), w_out, xs2, Bs * Ts, COL_TILE).reshape(Bs, Ts, D)

    new = (k.reshape(B, T, H, dh), v.reshape(B, T, H, dh), lf_p,
           ks.reshape(Bs, Ts, H, dh), vs.reshape(Bs, Ts, H, dh), lf_s)
    return yp_new, ys_new, new


def _gdn_gates_kernel(x_ref, al_ref, dt_ref, col_ref, row_ref, *, t_valid, nvh):
    x = x_ref[...]
    tb = x.shape[0]
    valid = pl.program_id(1) * tb + _iota(x.shape, 0) < t_valid
    beta = jnp.where(valid, _sigmoid(x), 0.0)
    g = jnp.where(valid, -jnp.exp(al_ref[...]) * _softplus(x + dt_ref[...]), 0.0)
    out = jnp.where(_iota(x.shape, 1) < nvh, beta, _chunk_row_cumsum(g, CHUNK))
    col_ref[...] = out
    row_ref[...] = out.T


def _gdn_gates(zt, a_log_l, dt_l, B, T, t_valid, nvh):
    tb = LANES
    nT = T // tb
    kern = functools.partial(_gdn_gates_kernel, t_valid=t_valid, nvh=nvh)
    return pl.pallas_call(
        kern,
        grid=(B, nT),
        in_specs=[pl.BlockSpec((tb, LANES), lambda b, t: (b * nT + t, 0)),
                  pl.BlockSpec((1, LANES), lambda b, t: (0, 0)),
                  pl.BlockSpec((1, LANES), lambda b, t: (0, 0))],
        out_specs=[pl.BlockSpec((tb, LANES), lambda b, t: (b * nT + t, 0)),
                   pl.BlockSpec((None, LANES, tb), lambda b, t: (b, 0, t))],
        out_shape=[jax.ShapeDtypeStruct((B * T, LANES), F32), jax.ShapeDtypeStruct((B, LANES, T), F32)],
        compiler_params=_params("parallel", "parallel"),
        name="gdn_gates",
    )(zt, a_log_l, dt_l)


def _gdn_conv_kernel(z_ref, w_ref, init_ref, o_ref, ext_ref, *, n_qk_tiles, dh):
    c = pl.program_id(1)
    tb = z_ref.shape[0]

    @pl.when(pl.program_id(2) == 0)
    def _():
        ext_ref[0:8, :] = init_ref[...]

    ext_ref[8:8 + tb, :] = z_ref[...]
    w = w_ref[...]
    y = (w[0:1, :] * ext_ref[5:5 + tb, :] + w[1:2, :] * ext_ref[6:6 + tb, :]
         + w[2:3, :] * ext_ref[7:7 + tb, :] + w[3:4, :] * ext_ref[8:8 + tb, :])
    y = y * _sigmoid(y)
    ext_ref[0:8, :] = ext_ref[tb:tb + 8, :]

    def l2(scale):
        for h in range(y.shape[1] // dh):
            sl = slice(h * dh, (h + 1) * dh)
            yh = y[:, sl]
            o_ref[:, sl] = yh * (lax.rsqrt(jnp.sum(yh * yh, axis=-1, keepdims=True) + EPS) * scale)

    @pl.when(c < n_qk_tiles)
    def _():
        l2(dh ** -0.5)

    @pl.when(jnp.logical_and(c >= n_qk_tiles, c < 2 * n_qk_tiles))
    def _():
        l2(1.0)

    @pl.when(c >= 2 * n_qk_tiles)
    def _():
        o_ref[...] = y


def _gdn_conv(z, conv_w, init, B, T, conv_dim, key_dim, dh, tb, tc):
    nT = T // tb
    kern = functools.partial(_gdn_conv_kernel, n_qk_tiles=key_dim // tc, dh=dh)
    return pl.pallas_call(
        kern,
        grid=(B, conv_dim // tc, nT),
        in_specs=[pl.BlockSpec((tb, tc), lambda b, c, t: (b * nT + t, c)),
                  pl.BlockSpec((conv_w.shape[0], tc), lambda b, c, t: (0, c)),
                  pl.BlockSpec((None, 8, tc), lambda b, c, t: (b, 0, c))],
        out_specs=pl.BlockSpec((tb, tc), lambda b, c, t: (b * nT + t, c)),
        out_shape=jax.ShapeDtypeStruct((B * T, conv_dim), F32),
        scratch_shapes=[pltpu.VMEM((tb + 8, tc), F32)],
        compiler_params=_params("parallel", "parallel", "arbitrary"),
        name="gdn_conv",
    )(z, conv_w, init)


def _tril_levels(n, chunk):
    i = lax.broadcasted_iota(jnp.int32, (n, n), 0)
    j = lax.broadcasted_iota(jnp.int32, (n, n), 1)
    lvl = jnp.full((n, n), -1, jnp.int32)
    s = 1
    e = 0
    while s < chunk:
        hit = ((i // (2 * s)) == (j // (2 * s))) & ((i // s) % 2 == 1) & ((j // s) % 2 == 0)
        lvl = jnp.where(hit, e, lvl)
        s *= 2
        e += 1
    return lvl


def _gdn_chunk_kernel(q_ref, k_ref, v_ref, zg_ref, col_ref, grow_ref, ng_ref, lvl_ref, s0_ref,
                      og_ref, sout_ref, s_ref, *, nvh, dh, rep):
    t = pl.program_id(2)
    L = CHUNK
    tb = q_ref.shape[0]
    nchunk = tb // L
    nhq = q_ref.shape[1] // dh

    @pl.when(t == 0)
    def _():
        s_ref[...] = s0_ref[...]

    colb = col_ref[...]
    lvl = lvl_ref[...]
    ri = _iota((tb, tb), 0)
    ci = _iota((tb, tb), 1)
    same = (ri // L) == (ci // L)
    causal = jnp.logical_and(same, ci <= ri)
    strict = jnp.logical_and(same, ci < ri)
    n = rep * tb
    eye = (_iota((n, n), 0) == _iota((n, n), 1)).astype(F32)
    zero = jnp.zeros((tb, tb), F32)

    heads = range(nhq)
    qs = [q_ref[:, hh * dh:(hh + 1) * dh] for hh in heads]
    ks = [k_ref[:, hh * dh:(hh + 1) * dh] for hh in heads]
    g_cols, qkd, m_full, rhs = [], [], [], []
    for hh in heads:
        hq = pl.program_id(1) * nhq + hh
        kk = _mm_nt(ks[hh], ks[hh])
        qk = _mm_nt(qs[hh], ks[hh])
        m_blocks, rhs_h = [], []
        for r in range(rep):
            vh = hh * rep + r
            beta_c = _pick_lane(colb, rep * hq + r)
            g_c = _pick_lane(colb, nvh + rep * hq + r)
            g_r = grow_ref[vh:vh + 1, :]
            decay = jnp.where(causal, jnp.exp(jnp.where(causal, g_c - g_r, 0.0)), 0.0)
            m_blocks.append(jnp.where(strict, beta_c * kk * decay, 0.0))
            qkd.append(qk * decay)
            g_cols.append(g_c)
            kb = ks[hh] * beta_c
            rhs_h.append(jnp.concatenate([v_ref[:, vh * dh:(vh + 1) * dh] * beta_c, kb * jnp.exp(g_c)], axis=1))
        m_full.append(jnp.concatenate(
            [jnp.concatenate([m_blocks[r] if rr == r else zero for rr in range(rep)], axis=1) for r in range(rep)],
            axis=0))
        rhs.append(jnp.concatenate(rhs_h, axis=0))

    tinv = [eye - jnp.where(lvl == 0, m_full[hh], 0.0) for hh in heads]
    s, e = 2, 1
    while s < L:
        half = [_mm(tinv[hh], jnp.where(lvl == e, m_full[hh], 0.0)) for hh in heads]
        tinv = [tinv[hh] - _mm(half[hh], tinv[hh]) for hh in heads]
        s *= 2
        e += 1
    sol = [_mm(tinv[hh], rhs[hh]) for hh in heads]

    for c in range(nchunk):
        sl = slice(c * L, (c + 1) * L)
        for vh in range(nhq * rep):
            hh, r = vh // rep, vh % rep
            g_c = g_cols[vh][sl]
            u = sol[hh][r * tb + c * L:r * tb + (c + 1) * L, :dh]
            w = sol[hh][r * tb + c * L:r * tb + (c + 1) * L, dh:]
            S = s_ref[vh]
            v_new = u - _mm(w, S)
            o = _mm(qs[hh][sl] * jnp.exp(g_c), S) + _mm(qkd[vh][sl, sl], v_new)
            g_last = g_c[L - 1:L, :]
            k_dec = ks[hh][sl] * jnp.exp(g_last - g_c)
            s_ref[vh] = S * jnp.exp(g_last) + _mm_tn(k_dec, v_new)
            zg = zg_ref[sl, vh * dh:(vh + 1) * dh]
            og_ref[sl, vh * dh:(vh + 1) * dh] = (_rms(o, ng_ref[...]) * (zg * _sigmoid(zg))).astype(BF16)

    @pl.when(t == pl.num_programs(2) - 1)
    def _():
        sout_ref[...] = s_ref[...]


def _gdn_chunk(qkv, z, gcol, grow, norm_g, s0, B, T, nqk, nvh, dh):
    tb = LANES
    nT = T // tb
    rep = nvh // nqk
    nhq = GDN_HEADS_PER_STEP
    qw, vw = nhq * dh, nhq * rep * dh
    key_dim = nqk * dh
    grow4 = grow.reshape(B, LANES // (nhq * rep), nhq * rep, T)
    kern = functools.partial(_gdn_chunk_kernel, nvh=nvh, dh=dh, rep=rep)
    row = lambda off: (lambda b, h, t: (b * nT + t, off + h))
    return pl.pallas_call(
        kern,
        grid=(B, nqk // nhq, nT),
        in_specs=[pl.BlockSpec((tb, qw), row(0)),
                  pl.BlockSpec((tb, qw), row(key_dim // qw)),
                  pl.BlockSpec((tb, vw), row(2 * key_dim // vw)),
                  pl.BlockSpec((tb, vw), row((2 * key_dim + nvh * dh) // vw)),
                  pl.BlockSpec((tb, LANES), lambda b, h, t: (b * nT + t, 0)),
                  pl.BlockSpec((None, None, nhq * rep, tb), lambda b, h, t: (b, nvh // (nhq * rep) + h, 0, t)),
                  pl.BlockSpec((1, dh), lambda b, h, t: (0, 0)),
                  pl.BlockSpec((rep * tb, rep * tb), lambda b, h, t: (0, 0)),
                  pl.BlockSpec((None, nhq * rep, dh, dh), lambda b, h, t: (b, h, 0, 0))],
        out_specs=[pl.BlockSpec((tb, vw), lambda b, h, t: (b * nT + t, h)),
                   pl.BlockSpec((None, nhq * rep, dh, dh), lambda b, h, t: (b, h, 0, 0))],
        out_shape=[jax.ShapeDtypeStruct((B * T, nvh * dh), BF16),
                   jax.ShapeDtypeStruct((B, nvh, dh, dh), F32)],
        scratch_shapes=[pltpu.VMEM((nhq * rep, dh, dh), F32)],
        compiler_params=_params("parallel", "parallel", "arbitrary"),
        name="gdn_chunk",
    )(qkv, qkv, qkv, z, gcol, grow4, norm_g, _tril_levels(rep * tb, CHUNK), s0)


def _pad_time(a, B, T, Tp):
    if Tp == T:
        return a
    return jnp.pad(a.reshape(B, T, -1), ((0, 0), (0, Tp - T), (0, 0))).reshape(B * Tp, -1)


def _gdn_mix(y, gain, conv_state, s0, w_in, conv_w, a_log, dt_bias, norm_g, w_out, nqk, tm):
    B, T, D = y.shape
    nvh = a_log.shape[0]
    dh = norm_g.shape[0]
    key_dim = nqk * dh
    conv_dim = 2 * key_dim + nvh * dh
    n_main = conv_dim + nvh * dh
    x2 = y.reshape(B * T, D)
    z, zt = _proj(x2, gain.reshape(1, D), w_in, n_main, _pad_cols(w_in[:, n_main:], LANES), tm, COL_TILE)
    Tp = -(-T // LANES) * LANES
    zp, ztp = _pad_time(z, B, T, Tp), _pad_time(zt, B, T, Tp)
    lane_par = lambda p: jnp.pad(p.reshape(1, nvh), ((0, 0), (nvh, LANES - 2 * nvh)))
    gcol, grow = _gdn_gates(ztp, lane_par(a_log), lane_par(dt_bias), B, Tp, T, nvh)
    init = jnp.pad(conv_state, ((0, 0), (8 - conv_state.shape[1], 0), (0, 0)))
    qkv = _gdn_conv(zp, conv_w, init, B, Tp, conv_dim, key_dim, dh, min(256, Tp), COL_TILE)
    og, s_new = _gdn_chunk(qkv, zp, gcol, grow, norm_g.reshape(1, dh), s0, B, Tp, nqk, nvh, dh)
    og = og.reshape(B, Tp, nvh * dh)[:, :T].reshape(B * T, nvh * dh)
    y_new = _oproj(og, w_out, x2, tm, COL_TILE).reshape(B, T, D)
    keep = conv_state.shape[1]
    last = z.reshape(B, T, -1)[:, T - min(T, keep):, :conv_dim]
    return y_new, jnp.concatenate([conv_state, last], axis=1)[:, -keep:], s_new


def _mlstm_gates_kernel(xi_ref, xf_ref, bi_ref, bf_ref, col_ref, row_ref, *, t_valid):
    xi = xi_ref[...]
    tb = xi.shape[0]
    valid = pl.program_id(1) * tb + _iota(xi.shape, 0) < t_valid
    ig = GATE_SOFTCAP * jnp.tanh((xi + bi_ref[...]) / GATE_SOFTCAP)
    lf = _log_sigmoid(GATE_SOFTCAP * jnp.tanh((xf_ref[...] + bf_ref[...]) / GATE_SOFTCAP))
    ig = jnp.where(valid, ig, NEG)
    a = _chunk_row_cumsum(jnp.where(valid, lf, 0.0), CHUNK)
    ia = ig - a
    col_ref[:, :LANES] = a
    col_ref[:, LANES:] = ia
    row_ref[...] = ia.T


def _mlstm_gates(zt, b_i_l, b_f_l, B, T, t_valid):
    tb = LANES
    nT = T // tb
    kern = functools.partial(_mlstm_gates_kernel, t_valid=t_valid)
    return pl.pallas_call(
        kern,
        grid=(B, nT),
        in_specs=[pl.BlockSpec((tb, LANES), lambda b, t: (b * nT + t, 0)),
                  pl.BlockSpec((tb, LANES), lambda b, t: (b * nT + t, 1)),
                  pl.BlockSpec((1, LANES), lambda b, t: (0, 0)),
                  pl.BlockSpec((1, LANES), lambda b, t: (0, 0))],
        out_specs=[pl.BlockSpec((tb, 2 * LANES), lambda b, t: (b * nT + t, 0)),
                   pl.BlockSpec((None, LANES, tb), lambda b, t: (b, 0, t))],
        out_shape=[jax.ShapeDtypeStruct((B * T, 2 * LANES), F32), jax.ShapeDtypeStruct((B, LANES, T), F32)],
        compiler_params=_params("parallel", "parallel"),
        name="mlstm_gates",
    )(zt, zt, b_i_l, b_f_l)


def _mlstm_chunk_kernel(q_ref, k_ref, v_ref, og_ref, col_ref, row_ref, ng_ref, c0_ref, n0_ref, m0_ref,
                        out_ref, cout_ref, nout_ref, mout_ref, c_ref, n_ref, m_ref, *, dqk):
    h = pl.program_id(1)
    t = pl.program_id(2)
    L = CHUNK
    tb = q_ref.shape[0]

    @pl.when(t == 0)
    def _():
        c_ref[...] = c0_ref[...]
        n_ref[...] = n0_ref[...]
        m_ref[...] = m0_ref[...]

    causal = _iota((L, L), 1) <= _iota((L, L), 0)
    for c in range(tb // L):
        sl = slice(c * L, (c + 1) * L)
        q = q_ref[sl, :]
        k = k_ref[sl, :] * (dqk ** -0.5)
        v = v_ref[sl, :]
        a_c = _pick_lane(col_ref[sl, :LANES], h)
        ia_c = _pick_lane(col_ref[sl, LANES:], h)
        ia_r = row_ref[:, sl]
        dmat = jnp.where(causal, a_c + ia_r, NEG)
        dmax = jnp.max(dmat, axis=1, keepdims=True)
        m = m_ref[...]
        inter = a_c + m
        m_row = jnp.maximum(inter, dmax)
        w_inter = jnp.exp(inter - m_row)
        smat = _mm_nt(q, k) * jnp.exp(dmat - m_row)
        C = c_ref[...]
        nvec = n_ref[...]
        num = w_inter * _mm(q, C) + _mm(smat, v)
        den = w_inter * jnp.sum(q * nvec, axis=1, keepdims=True) + jnp.sum(smat, axis=1, keepdims=True)
        hh = num / jnp.maximum(jnp.abs(den), jnp.exp(-m_row))
        m_new = m_row[L - 1:L, :]
        a_last = a_c[L - 1:L, :]
        scale = jnp.exp(a_last + m - m_new)
        kw = k * jnp.exp(a_last + ia_c - m_new)
        c_ref[...] = scale * C + _mm_tn(kw, v)
        n_ref[...] = scale * nvec + jnp.sum(kw, axis=0, keepdims=True)
        m_ref[...] = m_new
        out_ref[sl, :] = (_rms(hh, ng_ref[...]) * _sigmoid(og_ref[sl, :])).astype(BF16)

    @pl.when(t == pl.num_programs(2) - 1)
    def _():
        cout_ref[...] = c_ref[...]
        nout_ref[...] = n_ref[...]
        mout_ref[...] = m_ref[...]


def _mlstm_chunk(z, gcol, grow, norm_g, c0, n0, m0, B, T, nh, dqk, dv):
    tb = LANES
    nT = T // tb
    kern = functools.partial(_mlstm_chunk_kernel, dqk=dqk)
    row = lambda off: (lambda b, h, t: (b * nT + t, off + h))
    st = lambda b, h, t: (b, h, 0, 0)
    return pl.pallas_call(
        kern,
        grid=(B, nh, nT),
        in_specs=[pl.BlockSpec((tb, dqk), row(0)),
                  pl.BlockSpec((tb, dqk), row(nh)),
                  pl.BlockSpec((tb, dv), row(2 * nh * dqk // dv)),
                  pl.BlockSpec((tb, dv), row(2 * nh * dqk // dv + nh)),
                  pl.BlockSpec((tb, 2 * LANES), lambda b, h, t: (b * nT + t, 0)),
                  pl.BlockSpec((None, None, 1, tb), lambda b, h, t: (b, h, 0, t)),
                  pl.BlockSpec((None, 1, dv), lambda b, h, t: (h, 0, 0)),
                  pl.BlockSpec((None, None, dqk, dv), st),
                  pl.BlockSpec((None, None, 1, dqk), st),
                  pl.BlockSpec((None, None, 1, 1), st)],
        out_specs=[pl.BlockSpec((tb, dv), lambda b, h, t: (b * nT + t, h)),
                   pl.BlockSpec((None, None, dqk, dv), st),
                   pl.BlockSpec((None, None, 1, dqk), st),
                   pl.BlockSpec((None, None, 1, 1), st)],
        out_shape=[jax.ShapeDtypeStruct((B * T, nh * dv), BF16),
                   jax.ShapeDtypeStruct((B, nh, dqk, dv), F32),
                   jax.ShapeDtypeStruct((B, nh, 1, dqk), F32),
                   jax.ShapeDtypeStruct((B, nh, 1, 1), F32)],
        scratch_shapes=[pltpu.VMEM((dqk, dv), F32), pltpu.VMEM((1, dqk), F32), pltpu.VMEM((1, 1), F32)],
        compiler_params=_params("parallel", "parallel", "arbitrary"),
        name="mlstm_chunk",
    )(z, z, z, z, gcol, grow.reshape(B, LANES, 1, T), norm_g, c0, n0, m0)


def _mlstm_mix(y, gain, c0, n0, m0, w_in, b_i, b_f, norm_g, w_out, tm):
    B, T, D = y.shape
    nh = b_i.shape[0]
    dqk = c0.shape[2]
    dv = c0.shape[3]
    n_main = 2 * nh * dqk + nh * dv + D
    x2 = y.reshape(B * T, D)
    w_gate = w_in[:, n_main:]
    w_tail = jnp.concatenate([_pad_cols(w_gate[:, :nh], LANES), _pad_cols(w_gate[:, nh:], LANES)], axis=1)
    z, zt = _proj(x2, gain.reshape(1, D), w_in, n_main, w_tail, tm, COL_TILE)
    Tp = -(-T // LANES) * LANES
    zp, ztp = _pad_time(z, B, T, Tp), _pad_time(zt, B, T, Tp)
    gcol, grow = _mlstm_gates(ztp, _pad_cols(b_i.reshape(1, nh), LANES), _pad_cols(b_f.reshape(1, nh), LANES),
                              B, Tp, T)
    og, c_new, n_new, m_new = _mlstm_chunk(zp, gcol, grow, norm_g.reshape(nh, 1, dv), c0,
                                           n0.reshape(B, nh, 1, dqk), m0.reshape(B, nh, 1, 1), B, Tp, nh, dqk, dv)
    og = og.reshape(B, Tp, nh * dv)[:, :T].reshape(B * T, nh * dv)
    y_new = _oproj(og, w_out, x2, tm, COL_TILE).reshape(B, T, D)
    return y_new, c_new, n_new.reshape(B, nh, dqk), m_new.reshape(B, nh)


def _mlp_layer(y, gain, w_up, w_down, tm):
    B, T, D = y.shape
    return _mlp(y.reshape(B * T, D), gain.reshape(1, D), w_up, w_down, tm, COL_TILE).reshape(B, T, D)


def kernel(x_prompt, x_sample, cache_k_l0, cache_v_l0, cache_logf_l0, state_conv_l1, state_ssm_l1, state_C_l2, state_n_l2, state_m_l2, cache_k_l3, cache_v_l3, cache_logf_l3, page_table, norm_mix, norm_mlp, w_up, w_down, fox_w_in, fox_b_f, fox_q_norm, fox_k_norm, fox_w_out, gdn_w_in, gdn_conv_w, gdn_A_log, gdn_dt_bias, gdn_norm, gdn_w_out, mlstm_w_in, mlstm_b_i, mlstm_b_f, mlstm_norm, mlstm_w_out):
    yp, ys = x_prompt, x_sample
    B, T, D = yp.shape
    Bs, Ts, _ = ys.shape
    tm_p = _row_tile(B * T, ROW_TILE)
    tm_s = Bs * Ts
    nqk = D // gdn_norm.shape[1]

    def mlp(i, yp, ys):
        return (_mlp_layer(yp, norm_mlp[i], w_up[i], w_down[i], tm_p),
                _mlp_layer(ys, norm_mlp[i], w_up[i], w_down[i], tm_s))

    yp, ys, fox0 = _fox_layer(yp, ys, cache_k_l0, cache_v_l0, cache_logf_l0, page_table, norm_mix[0],
                              fox_w_in[0], fox_b_f[0], fox_q_norm[0], fox_k_norm[0], fox_w_out[0])
    yp, ys = mlp(0, yp, ys)

    gw = (gdn_w_in[0], gdn_conv_w[0], gdn_A_log[0], gdn_dt_bias[0], gdn_norm[0], gdn_w_out[0])
    nvh, dh = state_ssm_l1.shape[1], state_ssm_l1.shape[2]
    yp, conv_p, ssm_p = _gdn_mix(yp, norm_mix[1], jnp.zeros((B,) + state_conv_l1.shape[1:], F32),
                                 jnp.zeros((B, nvh, dh, dh), F32), *gw, nqk, tm_p)
    ys, conv_s, ssm_s = _gdn_mix(ys, norm_mix[1], state_conv_l1, state_ssm_l1, *gw, nqk, tm_s)
    yp, ys = mlp(1, yp, ys)

    mw = (mlstm_w_in[0], mlstm_b_i[0], mlstm_b_f[0], mlstm_norm[0], mlstm_w_out[0])
    zc = lambda a: jnp.zeros((B,) + a.shape[1:], F32)
    yp, c_p, n_p, m_p = _mlstm_mix(yp, norm_mix[2], zc(state_C_l2), zc(state_n_l2), zc(state_m_l2), *mw, tm_p)
    ys, c_s, n_s, m_s = _mlstm_mix(ys, norm_mix[2], state_C_l2, state_n_l2, state_m_l2, *mw, tm_s)
    yp, ys = mlp(2, yp, ys)

    yp, ys, fox3 = _fox_layer(yp, ys, cache_k_l3, cache_v_l3, cache_logf_l3, page_table, norm_mix[3],
                              fox_w_in[1], fox_b_f[1], fox_q_norm[1], fox_k_norm[1], fox_w_out[1])
    yp, ys = mlp(3, yp, ys)

    return (yp, ys) + fox0 + (conv_p, ssm_p, conv_s, ssm_s, c_p, n_p, m_p, c_s, n_s, m_s) + fox3
```

```python
import functools

import jax
import jax.numpy as jnp
from jax import lax
from jax.experimental import pallas as pl
from jax.experimental.pallas import tpu as pltpu

F32 = jnp.float32
BF16 = jnp.bfloat16
EPS = 1e-6
NEG = -1e30
LANES = 128
CHUNK = 64
VMEM_LIMIT = 56 * 1024 * 1024
ROW_TILE = 1024
COL_TILE = 512
ATTN_BLOCK = 1024
ATTN_ROWS = 256
ATTN_KEYS = 512
DECODE_PAGES = 4
MLSTM_HEADS_PER_STEP = 4
GDN_HEADS_PER_STEP = 4
GATE_SOFTCAP = 15.0
_NT = (((1,), (1,)), ((), ()))
_TN = (((0,), (0,)), ((), ()))


def _params(*sem):
    return pltpu.CompilerParams(dimension_semantics=sem, vmem_limit_bytes=VMEM_LIMIT)


def _mm(a, b):
    return jnp.dot(a.astype(BF16), b.astype(BF16), preferred_element_type=F32)


def _mm_nt(a, b):
    return lax.dot_general(a.astype(BF16), b.astype(BF16), _NT, preferred_element_type=F32)


def _mm_tn(a, b):
    return lax.dot_general(a.astype(BF16), b.astype(BF16), _TN, preferred_element_type=F32)


def _sigmoid(x):
    return 1.0 / (1.0 + jnp.exp(-x))


def _softplus(x):
    return jnp.maximum(x, 0.0) + jnp.log(1.0 + jnp.exp(-jnp.abs(x)))


def _log_sigmoid(x):
    return -_softplus(-x)


def _rms(x, gain):
    return x * lax.rsqrt(jnp.mean(x * x, axis=-1, keepdims=True) + EPS) * gain


def _iota(shape, axis):
    return lax.broadcasted_iota(jnp.int32, shape, axis)


def _lane_cumsum(x):
    n = x.shape[-1]
    lane = _iota(x.shape, x.ndim - 1)
    s = 1
    while s < n:
        x = x + jnp.where(lane >= s, pltpu.roll(x, s, x.ndim - 1), 0.0)
        s *= 2
    return x


def _chunk_row_cumsum(x, chunk):
    pos = _iota(x.shape, 0) % chunk
    s = 1
    while s < chunk:
        x = x + jnp.where(pos >= s, pltpu.roll(x, s, 0), 0.0)
        s *= 2
    return x


def _pick_lane(block, lane_idx):
    return jnp.sum(jnp.where(_iota(block.shape, 1) == lane_idx, block, 0.0), axis=1, keepdims=True)


def _proj_kernel(x_ref, g_ref, w_ref, wt_ref, z_ref, zt_ref, xn_ref):
    @pl.when(pl.program_id(1) == 0)
    def _():
        xn = _rms(x_ref[...], g_ref[...]).astype(BF16)
        xn_ref[...] = xn
        zt_ref[...] = jnp.dot(xn, wt_ref[...].astype(BF16), preferred_element_type=F32)

    z_ref[...] = jnp.dot(xn_ref[...], w_ref[...].astype(BF16), preferred_element_type=F32)


def _layer_spec(w, block, index_map):
    layer = w[1]
    return pl.BlockSpec((None,) + block, lambda i, j: (layer,) + index_map(i, j))


def _proj(x, gain, w, N, w_tail, tm, tn):
    M, D = x.shape
    NT_ = w_tail.shape[1]
    return pl.pallas_call(
        _proj_kernel,
        grid=(M // tm, N // tn),
        in_specs=[pl.BlockSpec((tm, D), lambda i, j: (i, 0)),
                  pl.BlockSpec((1, D), lambda i, j: (0, 0)),
                  _layer_spec(w, (D, tn), lambda i, j: (0, j)),
                  pl.BlockSpec((D, NT_), lambda i, j: (0, 0))],
        out_specs=[pl.BlockSpec((tm, tn), lambda i, j: (i, j)),
                   pl.BlockSpec((tm, NT_), lambda i, j: (i, 0))],
        out_shape=[jax.ShapeDtypeStruct((M, N), F32), jax.ShapeDtypeStruct((M, NT_), F32)],
        scratch_shapes=[pltpu.VMEM((tm, D), BF16)],
        compiler_params=_params("parallel", "arbitrary"),
        name="proj",
    )(x, gain, w[0], w_tail)


def _oproj_kernel(a_ref, w_ref, r_ref, o_ref):
    o_ref[...] = r_ref[...] + jnp.dot(a_ref[...], w_ref[...].astype(BF16), preferred_element_type=F32)


def _oproj(a, w, resid, tm, tn):
    M, K = a.shape
    N = w[0].shape[2]
    return pl.pallas_call(
        _oproj_kernel,
        grid=(M // tm, N // tn),
        in_specs=[pl.BlockSpec((tm, K), lambda i, j: (i, 0)),
                  _layer_spec(w, (K, tn), lambda i, j: (0, j)),
                  pl.BlockSpec((tm, tn), lambda i, j: (i, j))],
        out_specs=pl.BlockSpec((tm, tn), lambda i, j: (i, j)),
        out_shape=jax.ShapeDtypeStruct((M, N), F32),
        compiler_params=_params("parallel", "arbitrary"),
        name="oproj",
    )(a, w[0], resid)


def _mlp_kernel(x_ref, g_ref, wu_ref, wd_ref, o_ref, xn_ref):
    @pl.when(pl.program_id(1) == 0)
    def _():
        x = x_ref[...]
        xn_ref[...] = _rms(x, g_ref[...]).astype(BF16)
        o_ref[...] = x

    h = jnp.dot(xn_ref[...], wu_ref[...].astype(BF16), preferred_element_type=F32)
    h = jnp.maximum(h, 0.0)
    o_ref[...] += jnp.dot((h * h).astype(BF16), wd_ref[...].astype(BF16), preferred_element_type=F32)


def _mlp(x, gain, w_up, w_down, tm, tf):
    M, D = x.shape
    Fdim = w_up[0].shape[2]
    return pl.pallas_call(
        _mlp_kernel,
        grid=(M // tm, Fdim // tf),
        in_specs=[pl.BlockSpec((tm, D), lambda i, j: (i, 0), pipeline_mode=pl.Buffered(1)),
                  pl.BlockSpec((1, D), lambda i, j: (0, 0)),
                  _layer_spec(w_up, (D, tf), lambda i, j: (0, j)),
                  _layer_spec(w_down, (tf, D), lambda i, j: (j, 0))],
        out_specs=pl.BlockSpec((tm, D), lambda i, j: (i, 0), pipeline_mode=pl.Buffered(1)),
        out_shape=jax.ShapeDtypeStruct((M, D), F32),
        scratch_shapes=[pltpu.VMEM((tm, D), BF16)],
        compiler_params=_params("parallel", "arbitrary"),
        name="mlp",
    )(x, gain, w_up[0], w_down[0])


def _fox_proj_kernel(x_ref, g_ref, w_ref, wt_ref, bt_ref, qg_ref, kg_ref,
                     q_ref, k_ref, kb_ref, v_ref, vb_ref, gate_ref, lf_ref, xn_ref, *, nseg, dh):
    j = pl.program_id(1)

    @pl.when(j == 0)
    def _():
        xn = _rms(x_ref[...], g_ref[...]).astype(BF16)
        xn_ref[...] = xn
        zt = jnp.dot(xn, wt_ref[...].astype(BF16), preferred_element_type=F32) + bt_ref[...]
        lf_ref[...] = _log_sigmoid(zt)

    acc = jnp.dot(xn_ref[...], w_ref[...].astype(BF16), preferred_element_type=F32)
    heads = acc.shape[1] // dh
    seg = j // nseg

    @pl.when(seg == 0)
    def _():
        for h in range(heads):
            sl = slice(h * dh, (h + 1) * dh)
            q_ref[:, sl] = _rms(acc[:, sl], qg_ref[...]).astype(BF16)

    @pl.when(seg == 1)
    def _():
        for h in range(heads):
            sl = slice(h * dh, (h + 1) * dh)
            kn = _rms(acc[:, sl], kg_ref[...])
            k_ref[:, sl] = kn
            kb_ref[:, sl] = kn.astype(BF16)

    @pl.when(seg == 2)
    def _():
        v_ref[...] = acc
        vb_ref[...] = acc.astype(BF16)

    @pl.when(seg == 3)
    def _():
        gate_ref[...] = acc


def _fox_proj(x, gain, w_main, w_tail, b_tail, q_gain, k_gain, tm, tn):
    M, D = x.shape
    dh = q_gain.shape[1]
    nseg = D // tn

    def seg_map(s):
        return lambda i, j: (i, jnp.clip(j - s * nseg, 0, nseg - 1))

    row = lambda i, j: (i, 0)
    const = lambda i, j: (0, 0)
    kern = functools.partial(_fox_proj_kernel, nseg=nseg, dh=dh)
    return pl.pallas_call(
        kern,
        grid=(M // tm, 4 * nseg),
        in_specs=[pl.BlockSpec((tm, D), row),
                  pl.BlockSpec((1, D), const),
                  _layer_spec(w_main, (D, tn), lambda i, j: (0, j)),
                  pl.BlockSpec((D, LANES), const),
                  pl.BlockSpec((1, LANES), const),
                  pl.BlockSpec((1, dh), const),
                  pl.BlockSpec((1, dh), const)],
        out_specs=[pl.BlockSpec((tm, tn), seg_map(0)),
                   pl.BlockSpec((tm, tn), seg_map(1)),
                   pl.BlockSpec((tm, tn), seg_map(1)),
                   pl.BlockSpec((tm, tn), seg_map(2)),
                   pl.BlockSpec((tm, tn), seg_map(2)),
                   pl.BlockSpec((tm, tn), seg_map(3)),
                   pl.BlockSpec((tm, LANES), row)],
        out_shape=[jax.ShapeDtypeStruct((M, D), BF16),
                   jax.ShapeDtypeStruct((M, D), F32),
                   jax.ShapeDtypeStruct((M, D), BF16),
                   jax.ShapeDtypeStruct((M, D), F32),
                   jax.ShapeDtypeStruct((M, D), BF16),
                   jax.ShapeDtypeStruct((M, D), F32),
                   jax.ShapeDtypeStruct((M, LANES), F32)],
        scratch_shapes=[pltpu.VMEM((tm, D), BF16)],
        compiler_params=_params("parallel", "arbitrary"),
        name="fox_proj",
    )(x, gain, w_main[0], w_tail, b_tail, q_gain, k_gain)


def _cumsum_kernel(x_ref, o_ref):
    o_ref[...] = _lane_cumsum(x_ref[...])


def _time_cumsum(x):
    B, H, T = x.shape
    return pl.pallas_call(
        _cumsum_kernel,
        grid=(B,),
        in_specs=[pl.BlockSpec((None, H, T), lambda b: (b, 0, 0))],
        out_specs=pl.BlockSpec((None, H, T), lambda b: (b, 0, 0)),
        out_shape=jax.ShapeDtypeStruct((B, H, T), F32),
        compiler_params=_params("parallel"),
        name="time_cumsum",
    )(x)


def _fox_attn_kernel(qi_ref, ki_ref, q_ref, k_ref, v_ref, ck_ref, g_ref, o_ref, m_ref, acc_ref,
                     *, scale, rows, keys):
    qi = qi_ref[pl.program_id(2)]
    ki = ki_ref[pl.program_id(2)]
    blk, dh = q_ref.shape

    @pl.when(ki == 0)
    def _():
        m_ref[...] = jnp.full(m_ref.shape, NEG, F32)
        acc_ref[...] = jnp.zeros(acc_ref.shape, F32)

    def update(on_diagonal):
        v_ext = jnp.concatenate([v_ref[...], jnp.ones((blk, dh), BF16)], axis=1)
        for r0 in range(0, blk, rows):
            rs = slice(r0, r0 + rows)
            k_end = r0 + rows if on_diagonal else blk
            m_run = m_ref[rs, :]
            acc = acc_ref[rs, :]
            for c0 in range(0, k_end, keys):
                cs = slice(c0, min(c0 + keys, k_end))
                s = lax.dot_general(q_ref[rs, :], k_ref[cs, :], _NT, preferred_element_type=F32) * scale
                s = s - ck_ref[:, cs]
                if on_diagonal and cs.stop > r0 + 1:
                    s = jnp.where(_iota(s.shape, 1) + c0 <= _iota(s.shape, 0) + r0, s, NEG)
                m_new = jnp.maximum(m_run, jnp.max(s, axis=1, keepdims=True))
                p = jnp.exp(s - m_new).astype(BF16)
                acc = jnp.exp(m_run - m_new) * acc + jnp.dot(p, v_ext[cs, :], preferred_element_type=F32)
                m_run = m_new
            m_ref[rs, :] = m_run
            acc_ref[rs, :] = acc

    @pl.when(ki < qi)
    def _():
        update(False)

    @pl.when(ki == qi)
    def _():
        update(True)
        o = acc_ref[:, :dh] / acc_ref[:, dh:] * _sigmoid(g_ref[...])
        o_ref[...] = o.astype(BF16)


def _fox_attn(q, kb, vb, cum_t, gate, B, T, H, dh, blk):
    M, D = q.shape
    nb = T // blk
    pairs = [(qi, ki) for qi in range(nb) for ki in range(qi + 1)]
    qi_tab = jnp.asarray([p[0] for p in pairs], jnp.int32)
    ki_tab = jnp.asarray([p[1] for p in pairs], jnp.int32)
    kern = functools.partial(_fox_attn_kernel, scale=dh ** -0.5, rows=min(ATTN_ROWS, blk), keys=min(ATTN_KEYS, blk))
    qmap = lambda b, h, s, qt, kt: (b * nb + qt[s], h)
    kmap = lambda b, h, s, qt, kt: (b * nb + kt[s], h)
    grid_spec = pltpu.PrefetchScalarGridSpec(
        num_scalar_prefetch=2,
        grid=(B, H, len(pairs)),
        in_specs=[pl.BlockSpec((blk, dh), qmap),
                  pl.BlockSpec((blk, dh), kmap),
                  pl.BlockSpec((blk, dh), kmap),
                  pl.BlockSpec((None, 1, blk), lambda b, h, s, qt, kt: (b * H + h, 0, kt[s])),
                  pl.BlockSpec((blk, dh), qmap)],
        out_specs=pl.BlockSpec((blk, dh), qmap),
        scratch_shapes=[pltpu.VMEM((blk, 1), F32), pltpu.VMEM((blk, 2 * dh), F32)])
    return pl.pallas_call(
        kern,
        grid_spec=grid_spec,
        out_shape=jax.ShapeDtypeStruct((M, D), BF16),
        compiler_params=_params("parallel", "parallel", "arbitrary"),
        name="fox_attn",
    )(qi_tab, ki_tab, q, kb, vb, cum_t, gate)


def _fox_decode_kernel(pt_ref, q_ref, *refs, nsteps, group, nh, nt, dh, scale):
    kp_refs, vp_refs, lfp_refs = refs[:group], refs[group:2 * group], refs[2 * group:3 * group]
    kn_ref, vn_ref, lfn_ref, g_ref, o_ref, qbd_ref, m_ref, l_ref, acc_ref, coff_ref = refs[3 * group:]
    p_id = pl.program_id(1)
    R = nt * nh
    D = nh * dh

    def head_mask():
        return (_iota((R, D), 0) % nh) == (_iota((R, D), 1) // dh)

    @pl.when(p_id == 0)
    def _():
        q = q_ref[...]
        rows = jnp.concatenate([jnp.broadcast_to(q[t:t + 1, :], (nh, D)) for t in range(nt)], axis=0)
        qbd_ref[...] = jnp.where(head_mask(), rows, 0.0).astype(BF16)
        m_ref[...] = jnp.full(m_ref.shape, NEG, F32)
        l_ref[...] = jnp.zeros(l_ref.shape, F32)
        acc_ref[...] = jnp.zeros(acc_ref.shape, F32)
        coff_ref[...] = jnp.zeros(coff_ref.shape, F32)

    def step(k, v, lf_t, causal):
        tot = coff_ref[...] + _lane_cumsum(lf_t)
        bias = jnp.concatenate([tot] * nt, axis=0)
        s = lax.dot_general(qbd_ref[...], k.astype(BF16), _NT, preferred_element_type=F32) * scale - bias
        if causal:
            s = jnp.where(_iota(s.shape, 1) <= _iota(s.shape, 0) // nh, s, NEG)
        m_prev = m_ref[...]
        m_new = jnp.maximum(m_prev, jnp.max(s, axis=1, keepdims=True))
        alpha = jnp.exp(m_prev - m_new)
        p = jnp.exp(s - m_new)
        l_ref[...] = alpha * l_ref[...] + jnp.sum(p, axis=1, keepdims=True)
        acc_ref[...] = alpha * acc_ref[...] + jnp.dot(p.astype(BF16), v.astype(BF16), preferred_element_type=F32)
        m_ref[...] = m_new
        coff_ref[...] = tot[:, tot.shape[1] - 1:]

    @pl.when(p_id < nsteps)
    def _():
        page = lfp_refs[0].shape[1]
        heads = lambda ref: jnp.concatenate([ref[pl.ds(h, page, stride=nh), :] for h in range(nh)], axis=1)
        gather = lambda page_refs: jnp.concatenate([heads(r) for r in page_refs], axis=0)
        step(gather(kp_refs), gather(vp_refs), jnp.concatenate([r[...] for r in lfp_refs], axis=1), False)

    @pl.when(p_id == nsteps)
    def _():
        step(kn_ref[...], vn_ref[...], lfn_ref[...], True)
        o_full = jnp.where(head_mask(), acc_ref[...] / l_ref[...], 0.0)
        o = jnp.concatenate([jnp.sum(o_full[t * nh:(t + 1) * nh, :], axis=0, keepdims=True) for t in range(nt)],
                            axis=0)
        o_ref[...] = o * _sigmoid(g_ref[...])


def _fox_decode(page_table, q, cache_k, cache_v, cache_lf_t, k_new, v_new, lf_new_t, gate, nh, dh):
    B, nt, D = q.shape
    npages = page_table.shape[1]
    page = cache_k.shape[1] // nh
    group = DECODE_PAGES if npages % DECODE_PAGES == 0 else 1
    nsteps = npages // group
    kern = functools.partial(_fox_decode_kernel, nsteps=nsteps, group=group, nh=nh, nt=nt, dh=dh, scale=dh ** -0.5)

    def pmap(i):
        return lambda b, p, pt: (pt[b, jnp.minimum(p, nsteps - 1) * group + i], 0, 0)

    bmap = lambda b, p, pt: (b, 0, 0)
    R = nt * nh
    grid_spec = pltpu.PrefetchScalarGridSpec(
        num_scalar_prefetch=1,
        grid=(B, nsteps + 1),
        in_specs=[pl.BlockSpec((None, nt, D), bmap)]
                 + [pl.BlockSpec((None, page * nh, dh), pmap(i)) for i in range(group)]
                 + [pl.BlockSpec((None, page * nh, dh), pmap(i)) for i in range(group)]
                 + [pl.BlockSpec((None, nh, page), pmap(i)) for i in range(group)]
                 + [pl.BlockSpec((None, page, D), bmap),
                  pl.BlockSpec((None, page, D), bmap),
                  pl.BlockSpec((None, nh, page), bmap),
                  pl.BlockSpec((None, nt, D), bmap)],
        out_specs=pl.BlockSpec((None, nt, D), bmap),
        scratch_shapes=[pltpu.VMEM((R, D), BF16), pltpu.VMEM((R, 1), F32), pltpu.VMEM((R, 1), F32),
                        pltpu.VMEM((R, D), F32), pltpu.VMEM((nh, 1), F32)])
    return pl.pallas_call(
        kern,
        grid_spec=grid_spec,
        out_shape=jax.ShapeDtypeStruct((B, nt, D), F32),
        compiler_params=_params("parallel", "arbitrary"),
        name="fox_decode",
    )(page_table, q, *([cache_k] * group + [cache_v] * group + [cache_lf_t] * group), k_new, v_new, lf_new_t, gate)


def _pad_cols(a, n):
    return jnp.pad(a, ((0, 0), (0, n - a.shape[1])))


def _layer_cols(w, start):
    return w[0][w[1], :, start:]


def _row_tile(m, pref):
    return pref if m % pref == 0 else m


def _fox_layer(yp, ys, cache_k, cache_v, cache_logf, page_table, gain, w_in, b_f, q_gain, k_gain, w_out):
    B, T, D = yp.shape
    Bs, Ts, _ = ys.shape
    H = b_f.shape[0]
    dh = D // H
    w_main = w_in
    w_tail = _pad_cols(_layer_cols(w_in, 4 * D), LANES)
    b_tail = _pad_cols(b_f.reshape(1, H), LANES)
    gain = gain.reshape(1, D)
    qg = q_gain.reshape(1, dh)
    kg = k_gain.reshape(1, dh)

    x2 = yp.reshape(B * T, D)
    tm = _row_tile(B * T, ROW_TILE)
    q, k, kb, v, vb, gate, lf = _fox_proj(x2, gain, w_main, w_tail, b_tail, qg, kg, tm, COL_TILE)
    lf_p = lf[:, :H].reshape(B, T, H)
    cum_t = _time_cumsum(jnp.swapaxes(lf_p, 1, 2))
    og = _fox_attn(q, kb, vb, cum_t.reshape(B * H, 1, T), gate, B, T, H, dh, min(ATTN_BLOCK, T))
    yp_new = _oproj(og, w_out, x2, tm, COL_TILE).reshape(B, T, D)

    xs2 = ys.reshape(Bs * Ts, D)
    qs, ks, _, vs, _, gate_s, lfs = _fox_proj(xs2, gain, w_main, w_tail, b_tail, qg, kg, Bs * Ts, COL_TILE)
    n_pool, page = cache_k.shape[0], cache_k.shape[1]
    lf_s = lfs[:, :H].reshape(Bs, Ts, H)
    pad_rows = lambda a: jnp.pad(a.reshape(Bs, Ts, D), ((0, 0), (0, page - Ts), (0, 0)))
    lf_new_t = jnp.pad(jnp.swapaxes(lf_s, 1, 2), ((0, 0), (0, 0), (0, page - Ts)))
    ogs = _fox_decode(page_table, qs.astype(F32).reshape(Bs, Ts, D),
                      cache_k.reshape(n_pool, page * H, dh), cache_v.reshape(n_pool, page * H, dh),
                      jnp.swapaxes(cache_logf, 1, 2), pad_rows(ks), pad_rows(vs), lf_new_t,
                      gate_s.reshape(Bs, Ts, D), H, dh)
    ys_new = _oproj(ogs.reshape(Bs * Ts, D).astype(BF16), w_out, xs2, Bs * Ts, COL_TILE).reshape(Bs, Ts, D)

    new = (k.reshape(B, T, H, dh), v.reshape(B, T, H, dh), lf_p,
           ks.reshape(Bs, Ts, H, dh), vs.reshape(Bs, Ts, H, dh), lf_s)
    return yp_new, ys_new, new


def _gdn_gates_kernel(x_ref, al_ref, dt_ref, col_ref, row_ref, *, t_valid, nvh):
    x = x_ref[...]
    tb = x.shape[0]
    valid = pl.program_id(1) * tb + _iota(x.shape, 0) < t_valid
    beta = jnp.where(valid, _sigmoid(x), 0.0)
    g = jnp.where(valid, -jnp.exp(al_ref[...]) * _softplus(x + dt_ref[...]), 0.0)
    out = jnp.where(_iota(x.shape, 1) < nvh, beta, _chunk_row_cumsum(g, CHUNK))
    col_ref[...] = out
    row_ref[...] = out.T


def _gdn_gates(zt, a_log_l, dt_l, B, T, t_valid, nvh):
    tb = LANES
    nT = T // tb
    kern = functools.partial(_gdn_gates_kernel, t_valid=t_valid, nvh=nvh)
    return pl.pallas_call(
        kern,
        grid=(B, nT),
        in_specs=[pl.BlockSpec((tb, LANES), lambda b, t: (b * nT + t, 0)),
                  pl.BlockSpec((1, LANES), lambda b, t: (0, 0)),
                  pl.BlockSpec((1, LANES), lambda b, t: (0, 0))],
        out_specs=[pl.BlockSpec((tb, LANES), lambda b, t: (b * nT + t, 0)),
                   pl.BlockSpec((None, LANES, tb), lambda b, t: (b, 0, t))],
        out_shape=[jax.ShapeDtypeStruct((B * T, LANES), F32), jax.ShapeDtypeStruct((B, LANES, T), F32)],
        compiler_params=_params("parallel", "parallel"),
        name="gdn_gates",
    )(zt, a_log_l, dt_l)


def _gdn_conv_kernel(z_ref, w_ref, init_ref, o_ref, ext_ref, *, n_qk_tiles, dh):
    c = pl.program_id(1)
    tb = z_ref.shape[0]

    @pl.when(pl.program_id(2) == 0)
    def _():
        ext_ref[0:8, :] = init_ref[...]

    ext_ref[8:8 + tb, :] = z_ref[...]
    w = w_ref[...]
    y = (w[0:1, :] * ext_ref[5:5 + tb, :] + w[1:2, :] * ext_ref[6:6 + tb, :]
         + w[2:3, :] * ext_ref[7:7 + tb, :] + w[3:4, :] * ext_ref[8:8 + tb, :])
    y = y * _sigmoid(y)
    ext_ref[0:8, :] = ext_ref[tb:tb + 8, :]

    def l2(scale):
        for h in range(y.shape[1] // dh):
            sl = slice(h * dh, (h + 1) * dh)
            yh = y[:, sl]
            o_ref[:, sl] = yh * (lax.rsqrt(jnp.sum(yh * yh, axis=-1, keepdims=True) + EPS) * scale)

    @pl.when(c < n_qk_tiles)
    def _():
        l2(dh ** -0.5)

    @pl.when(jnp.logical_and(c >= n_qk_tiles, c < 2 * n_qk_tiles))
    def _():
        l2(1.0)

    @pl.when(c >= 2 * n_qk_tiles)
    def _():
        o_ref[...] = y


def _gdn_conv(z, conv_w, init, B, T, conv_dim, key_dim, dh, tb, tc):
    nT = T // tb
    kern = functools.partial(_gdn_conv_kernel, n_qk_tiles=key_dim // tc, dh=dh)
    return pl.pallas_call(
        kern,
        grid=(B, conv_dim // tc, nT),
        in_specs=[pl.BlockSpec((tb, tc), lambda b, c, t: (b * nT + t, c)),
                  pl.BlockSpec((conv_w.shape[0], tc), lambda b, c, t: (0, c)),
                  pl.BlockSpec((None, 8, tc), lambda b, c, t: (b, 0, c))],
        out_specs=pl.BlockSpec((tb, tc), lambda b, c, t: (b * nT + t, c)),
        out_shape=jax.ShapeDtypeStruct((B * T, conv_dim), F32),
        scratch_shapes=[pltpu.VMEM((tb + 8, tc), F32)],
        compiler_params=_params("parallel", "parallel", "arbitrary"),
        name="gdn_conv",
    )(z, conv_w, init)


def _tril_levels(n, chunk):
    i = lax.broadcasted_iota(jnp.int32, (n, n), 0)
    j = lax.broadcasted_iota(jnp.int32, (n, n), 1)
    lvl = jnp.full((n, n), -1, jnp.int32)
    s = 1
    e = 0
    while s < chunk:
        hit = ((i // (2 * s)) == (j // (2 * s))) & ((i // s) % 2 == 1) & ((j // s) % 2 == 0)
        lvl = jnp.where(hit, e, lvl)
        s *= 2
        e += 1
    return lvl


def _gdn_chunk_kernel(q_ref, k_ref, v_ref, zg_ref, col_ref, grow_ref, ng_ref, lvl_ref, s0_ref,
                      og_ref, sout_ref, s_ref, *, nvh, dh, rep):
    t = pl.program_id(2)
    L = CHUNK
    tb = q_ref.shape[0]
    nchunk = tb // L
    nhq = q_ref.shape[1] // dh

    @pl.when(t == 0)
    def _():
        s_ref[...] = s0_ref[...]

    colb = col_ref[...]
    lvl = lvl_ref[...]
    ri = _iota((tb, tb), 0)
    ci = _iota((tb, tb), 1)
    same = (ri // L) == (ci // L)
    causal = jnp.logical_and(same, ci <= ri)
    strict = jnp.logical_and(same, ci < ri)
    n = rep * tb
    eye = (_iota((n, n), 0) == _iota((n, n), 1)).astype(F32)
    zero = jnp.zeros((tb, tb), F32)

    heads = range(nhq)
    qs = [q_ref[:, hh * dh:(hh + 1) * dh] for hh in heads]
    ks = [k_ref[:, hh * dh:(hh + 1) * dh] for hh in heads]
    g_cols, qkd, m_full, rhs = [], [], [], []
    for hh in heads:
        hq = pl.program_id(1) * nhq + hh
        kk = _mm_nt(ks[hh], ks[hh])
        qk = _mm_nt(qs[hh], ks[hh])
        m_blocks, rhs_h = [], []
        for r in range(rep):
            vh = hh * rep + r
            beta_c = _pick_lane(colb, rep * hq + r)
            g_c = _pick_lane(colb, nvh + rep * hq + r)
            g_r = grow_ref[vh:vh + 1, :]
            decay = jnp.where(causal, jnp.exp(jnp.where(causal, g_c - g_r, 0.0)), 0.0)
            m_blocks.append(jnp.where(strict, beta_c * kk * decay, 0.0))
            qkd.append(qk * decay)
            g_cols.append(g_c)
            kb = ks[hh] * beta_c
            rhs_h.append(jnp.concatenate([v_ref[:, vh * dh:(vh + 1) * dh] * beta_c, kb * jnp.exp(g_c)], axis=1))
        m_full.append(jnp.concatenate(
            [jnp.concatenate([m_blocks[r] if rr == r else zero for rr in range(rep)], axis=1) for r in range(rep)],
            axis=0))
        rhs.append(jnp.concatenate(rhs_h, axis=0))

    tinv = [eye - jnp.where(lvl == 0, m_full[hh], 0.0) for hh in heads]
    s, e = 2, 1
    while s < L:
        half = [_mm(tinv[hh], jnp.where(lvl == e, m_full[hh], 0.0)) for hh in heads]
        tinv = [tinv[hh] - _mm(half[hh], tinv[hh]) for hh in heads]
        s *= 2
        e += 1
    sol = [_mm(tinv[hh], rhs[hh]) for hh in heads]

    for c in range(nchunk):
        sl = slice(c * L, (c + 1) * L)
        for vh in range(nhq * rep):
            hh, r = vh // rep, vh % rep
            g_c = g_cols[vh][sl]
            u = sol[hh][r * tb + c * L:r * tb + (c + 1) * L, :dh]
            w = sol[hh][r * tb + c * L:r * tb + (c + 1) * L, dh:]
            S = s_ref[vh]
            v_new = u - _mm(w, S)
            o = _mm(qs[hh][sl] * jnp.exp(g_c), S) + _mm(qkd[vh][sl, sl], v_new)
            g_last = g_c[L - 1:L, :]
            k_dec = ks[hh][sl] * jnp.exp(g_last - g_c)
            s_ref[vh] = S * jnp.exp(g_last) + _mm_tn(k_dec, v_new)
            zg = zg_ref[sl, vh * dh:(vh + 1) * dh]
            og_ref[sl, vh * dh:(vh + 1) * dh] = (_rms(o, ng_ref[...]) * (zg * _sigmoid(zg))).astype(BF16)

    @pl.when(t == pl.num_programs(2) - 1)
    def _():
        sout_ref[...] = s_ref[...]


def _gdn_chunk(qkv, z, gcol, grow, norm_g, s0, B, T, nqk, nvh, dh):
    tb = LANES
    nT = T // tb
    rep = nvh // nqk
    nhq = GDN_HEADS_PER_STEP
    qw, vw = nhq * dh, nhq * rep * dh
    key_dim = nqk * dh
    grow4 = grow.reshape(B, LANES // (nhq * rep), nhq * rep, T)
    kern = functools.partial(_gdn_chunk_kernel, nvh=nvh, dh=dh, rep=rep)
    row = lambda off: (lambda b, h, t: (b * nT + t, off + h))
    return pl.pallas_call(
        kern,
        grid=(B, nqk // nhq, nT),
        in_specs=[pl.BlockSpec((tb, qw), row(0)),
                  pl.BlockSpec((tb, qw), row(key_dim // qw)),
                  pl.BlockSpec((tb, vw), row(2 * key_dim // vw)),
                  pl.BlockSpec((tb, vw), row((2 * key_dim + nvh * dh) // vw)),
                  pl.BlockSpec((tb, LANES), lambda b, h, t: (b * nT + t, 0)),
                  pl.BlockSpec((None, None, nhq * rep, tb), lambda b, h, t: (b, nvh // (nhq * rep) + h, 0, t)),
                  pl.BlockSpec((1, dh), lambda b, h, t: (0, 0)),
                  pl.BlockSpec((rep * tb, rep * tb), lambda b, h, t: (0, 0)),
                  pl.BlockSpec((None, nhq * rep, dh, dh), lambda b, h, t: (b, h, 0, 0))],
        out_specs=[pl.BlockSpec((tb, vw), lambda b, h, t: (b * nT + t, h)),
                   pl.BlockSpec((None, nhq * rep, dh, dh), lambda b, h, t: (b, h, 0, 0))],
        out_shape=[jax.ShapeDtypeStruct((B * T, nvh * dh), BF16),
                   jax.ShapeDtypeStruct((B, nvh, dh, dh), F32)],
        scratch_shapes=[pltpu.VMEM((nhq * rep, dh, dh), F32)],
        compiler_params=_params("parallel", "parallel", "arbitrary"),
        name="gdn_chunk",
    )(qkv, qkv, qkv, z, gcol, grow4, norm_g, _tril_levels(rep * tb, CHUNK), s0)


def _pad_time(a, B, T, Tp):
    if Tp == T:
        return a
    return jnp.pad(a.reshape(B, T, -1), ((0, 0), (0, Tp - T), (0, 0))).reshape(B * Tp, -1)


def _unpad_time(a, B, T, Tp):
    if Tp == T:
        return a
    return a.reshape(B, Tp, -1)[:, :T].reshape(B * T, -1)


def _gdn_mix(y, gain, conv_state, s0, w_in, conv_w, a_log, dt_bias, norm_g, w_out, nqk, tm):
    B, T, D = y.shape
    nvh = a_log.shape[0]
    dh = norm_g.shape[0]
    key_dim = nqk * dh
    conv_dim = 2 * key_dim + nvh * dh
    n_main = conv_dim + nvh * dh
    x2 = y.reshape(B * T, D)
    z, zt = _proj(x2, gain.reshape(1, D), w_in, n_main, _pad_cols(_layer_cols(w_in, n_main), LANES), tm, COL_TILE)
    Tp = -(-T // LANES) * LANES
    zp, ztp = _pad_time(z, B, T, Tp), _pad_time(zt, B, T, Tp)
    lane_par = lambda p: jnp.pad(p.reshape(1, nvh), ((0, 0), (nvh, LANES - 2 * nvh)))
    gcol, grow = _gdn_gates(ztp, lane_par(a_log), lane_par(dt_bias), B, Tp, T, nvh)
    init = jnp.pad(conv_state, ((0, 0), (8 - conv_state.shape[1], 0), (0, 0)))
    qkv = _gdn_conv(zp, conv_w, init, B, Tp, conv_dim, key_dim, dh, min(256, Tp), COL_TILE)
    og, s_new = _gdn_chunk(qkv, zp, gcol, grow, norm_g.reshape(1, dh), s0, B, Tp, nqk, nvh, dh)
    og = _unpad_time(og, B, T, Tp)
    y_new = _oproj(og, w_out, x2, tm, COL_TILE).reshape(B, T, D)
    keep = conv_state.shape[1]
    n_last = min(T, keep)
    last = jnp.stack([lax.slice(z, (b * T + T - n_last, 0), (b * T + T, conv_dim)) for b in range(B)])
    return y_new, jnp.concatenate([conv_state, last], axis=1)[:, -keep:], s_new


def _mlstm_gates_kernel(xi_ref, xf_ref, bi_ref, bf_ref, col_ref, row_ref, *, t_valid):
    xi = xi_ref[...]
    tb = xi.shape[0]
    valid = pl.program_id(1) * tb + _iota(xi.shape, 0) < t_valid
    ig = GATE_SOFTCAP * jnp.tanh((xi + bi_ref[...]) / GATE_SOFTCAP)
    lf = _log_sigmoid(GATE_SOFTCAP * jnp.tanh((xf_ref[...] + bf_ref[...]) / GATE_SOFTCAP))
    ig = jnp.where(valid, ig, NEG)
    a = _chunk_row_cumsum(jnp.where(valid, lf, 0.0), CHUNK)
    ia = ig - a
    col_ref[:, :LANES] = a
    col_ref[:, LANES:] = ia
    row_ref[...] = ia.T


def _mlstm_gates(zt, b_i_l, b_f_l, B, T, t_valid):
    tb = LANES
    nT = T // tb
    kern = functools.partial(_mlstm_gates_kernel, t_valid=t_valid)
    return pl.pallas_call(
        kern,
        grid=(B, nT),
        in_specs=[pl.BlockSpec((tb, LANES), lambda b, t: (b * nT + t, 0)),
                  pl.BlockSpec((tb, LANES), lambda b, t: (b * nT + t, 1)),
                  pl.BlockSpec((1, LANES), lambda b, t: (0, 0)),
                  pl.BlockSpec((1, LANES), lambda b, t: (0, 0))],
        out_specs=[pl.BlockSpec((tb, 2 * LANES), lambda b, t: (b * nT + t, 0)),
                   pl.BlockSpec((None, LANES, tb), lambda b, t: (b, 0, t))],
        out_shape=[jax.ShapeDtypeStruct((B * T, 2 * LANES), F32), jax.ShapeDtypeStruct((B, LANES, T), F32)],
        compiler_params=_params("parallel", "parallel"),
        name="mlstm_gates",
    )(zt, zt, b_i_l, b_f_l)


def _mlstm_chunk_kernel(q_ref, k_ref, v_ref, og_ref, col_ref, row_ref, ng_ref, c0_ref, n0_ref, m0_ref,
                        out_ref, cout_ref, nout_ref, mout_ref, c_ref, n_ref, m_ref, *, dqk, dv):
    t = pl.program_id(2)
    L = CHUNK
    tb = q_ref.shape[0]
    nhs = c_ref.shape[0]

    @pl.when(t == 0)
    def _():
        c_ref[...] = c0_ref[...]
        n_ref[...] = n0_ref[...]
        m_ref[...] = m0_ref[...]

    causal = _iota((L, L), 1) <= _iota((L, L), 0)
    for c in range(tb // L):
        sl = slice(c * L, (c + 1) * L)
        for hh in range(nhs):
            h = pl.program_id(1) * nhs + hh
            qs = slice(hh * dqk, (hh + 1) * dqk)
            vs = slice(hh * dv, (hh + 1) * dv)
            q = q_ref[sl, qs]
            k = k_ref[sl, qs] * (dqk ** -0.5)
            v = v_ref[sl, vs]
            a_c = _pick_lane(col_ref[sl, :LANES], h)
            ia_c = _pick_lane(col_ref[sl, LANES:], h)
            ia_r = row_ref[hh:hh + 1, sl]
            dmat = jnp.where(causal, a_c + ia_r, NEG)
            dmax = jnp.max(dmat, axis=1, keepdims=True)
            m = m_ref[hh]
            inter = a_c + m
            m_row = jnp.maximum(inter, dmax)
            w_inter = jnp.exp(inter - m_row)
            smat = _mm_nt(q, k) * jnp.exp(dmat - m_row)
            C = c_ref[hh]
            nvec = n_ref[hh]
            num = w_inter * _mm(q, C) + _mm(smat, v)
            den = w_inter * jnp.sum(q * nvec, axis=1, keepdims=True) + jnp.sum(smat, axis=1, keepdims=True)
            hid = num / jnp.maximum(jnp.abs(den), jnp.exp(-m_row))
            m_new = m_row[L - 1:L, :]
            a_last = a_c[L - 1:L, :]
            scale = jnp.exp(a_last + m - m_new)
            kw = k * jnp.exp(a_last + ia_c - m_new)
            c_ref[hh] = scale * C + _mm_tn(kw, v)
            n_ref[hh] = scale * nvec + jnp.sum(kw, axis=0, keepdims=True)
            m_ref[hh] = m_new
            out_ref[sl, vs] = (_rms(hid, ng_ref[hh:hh + 1, :]) * _sigmoid(og_ref[sl, vs])).astype(BF16)

    @pl.when(t == pl.num_programs(2) - 1)
    def _():
        cout_ref[...] = c_ref[...]
        nout_ref[...] = n_ref[...]
        mout_ref[...] = m_ref[...]


def _mlstm_chunk(z, gcol, grow, norm_g, c0, n0, m0, B, T, nh, dqk, dv):
    tb = LANES
    nT = T // tb
    nhs = MLSTM_HEADS_PER_STEP
    qw, vw = nhs * dqk, nhs * dv
    kern = functools.partial(_mlstm_chunk_kernel, dqk=dqk, dv=dv)
    row = lambda off: (lambda b, h, t: (b * nT + t, off + h))
    st = lambda b, h, t: (b, h, 0, 0)
    return pl.pallas_call(
        kern,
        grid=(B, nh // nhs, nT),
        in_specs=[pl.BlockSpec((tb, qw), row(0)),
                  pl.BlockSpec((tb, qw), row(nh * dqk // qw)),
                  pl.BlockSpec((tb, vw), row(2 * nh * dqk // vw)),
                  pl.BlockSpec((tb, vw), row((2 * nh * dqk + nh * dv) // vw)),
                  pl.BlockSpec((tb, 2 * LANES), lambda b, h, t: (b * nT + t, 0)),
                  pl.BlockSpec((None, None, nhs, tb), lambda b, h, t: (b, h, 0, t)),
                  pl.BlockSpec((None, nhs, dv), lambda b, h, t: (h, 0, 0)),
                  pl.BlockSpec((None, nhs, dqk, dv), st),
                  pl.BlockSpec((None, nhs, 1, dqk), st),
                  pl.BlockSpec((None, nhs, 1, 1), st)],
        out_specs=[pl.BlockSpec((tb, vw), lambda b, h, t: (b * nT + t, h)),
                   pl.BlockSpec((None, nhs, dqk, dv), st),
                   pl.BlockSpec((None, nhs, 1, dqk), st),
                   pl.BlockSpec((None, nhs, 1, 1), st)],
        out_shape=[jax.ShapeDtypeStruct((B * T, nh * dv), BF16),
                   jax.ShapeDtypeStruct((B, nh, dqk, dv), F32),
                   jax.ShapeDtypeStruct((B, nh, 1, dqk), F32),
                   jax.ShapeDtypeStruct((B, nh, 1, 1), F32)],
        scratch_shapes=[pltpu.VMEM((nhs, dqk, dv), F32), pltpu.VMEM((nhs, 1, dqk), F32),
                        pltpu.VMEM((nhs, 1, 1), F32)],
        compiler_params=_params("parallel", "parallel", "arbitrary"),
        name="mlstm_chunk",
    )(z, z, z, z, gcol, grow.reshape(B, LANES // nhs, nhs, T), norm_g.reshape(nh // nhs, nhs, dv), c0, n0, m0)


def _mlstm_mix(y, gain, c0, n0, m0, w_in, b_i, b_f, norm_g, w_out, tm):
    B, T, D = y.shape
    nh = b_i.shape[0]
    dqk = c0.shape[2]
    dv = c0.shape[3]
    n_main = 2 * nh * dqk + nh * dv + D
    x2 = y.reshape(B * T, D)
    w_gate = _layer_cols(w_in, n_main)
    w_tail = jnp.concatenate([_pad_cols(w_gate[:, :nh], LANES), _pad_cols(w_gate[:, nh:], LANES)], axis=1)
    z, zt = _proj(x2, gain.reshape(1, D), w_in, n_main, w_tail, tm, COL_TILE)
    Tp = -(-T // LANES) * LANES
    zp, ztp = _pad_time(z, B, T, Tp), _pad_time(zt, B, T, Tp)
    gcol, grow = _mlstm_gates(ztp, _pad_cols(b_i.reshape(1, nh), LANES), _pad_cols(b_f.reshape(1, nh), LANES),
                              B, Tp, T)
    og, c_new, n_new, m_new = _mlstm_chunk(zp, gcol, grow, norm_g.reshape(nh, 1, dv), c0,
                                           n0.reshape(B, nh, 1, dqk), m0.reshape(B, nh, 1, 1), B, Tp, nh, dqk, dv)
    og = _unpad_time(og, B, T, Tp)
    y_new = _oproj(og, w_out, x2, tm, COL_TILE).reshape(B, T, D)
    return y_new, c_new, n_new.reshape(B, nh, dqk), m_new.reshape(B, nh)


def _mlp_layer(y, gain, w_up, w_down, tm):
    B, T, D = y.shape
    return _mlp(y.reshape(B * T, D), gain.reshape(1, D), w_up, w_down, tm, COL_TILE).reshape(B, T, D)


def kernel(x_prompt, x_sample, cache_k_l0, cache_v_l0, cache_logf_l0, state_conv_l1, state_ssm_l1, state_C_l2, state_n_l2, state_m_l2, cache_k_l3, cache_v_l3, cache_logf_l3, page_table, norm_mix, norm_mlp, w_up, w_down, fox_w_in, fox_b_f, fox_q_norm, fox_k_norm, fox_w_out, gdn_w_in, gdn_conv_w, gdn_A_log, gdn_dt_bias, gdn_norm, gdn_w_out, mlstm_w_in, mlstm_b_i, mlstm_b_f, mlstm_norm, mlstm_w_out):
    yp, ys = x_prompt, x_sample
    B, T, D = yp.shape
    Bs, Ts, _ = ys.shape
    tm_p = _row_tile(B * T, ROW_TILE)
    tm_s = Bs * Ts
    nqk = D // gdn_norm.shape[1]

    def mlp(i, yp, ys):
        return (_mlp_layer(yp, norm_mlp[i], (w_up, i), (w_down, i), tm_p),
                _mlp_layer(ys, norm_mlp[i], (w_up, i), (w_down, i), tm_s))

    yp, ys, fox0 = _fox_layer(yp, ys, cache_k_l0, cache_v_l0, cache_logf_l0, page_table, norm_mix[0],
                              (fox_w_in, 0), fox_b_f[0], fox_q_norm[0], fox_k_norm[0], (fox_w_out, 0))
    yp, ys = mlp(0, yp, ys)

    gw = ((gdn_w_in, 0), gdn_conv_w[0], gdn_A_log[0], gdn_dt_bias[0], gdn_norm[0], (gdn_w_out, 0))
    nvh, dh = state_ssm_l1.shape[1], state_ssm_l1.shape[2]
    yp, conv_p, ssm_p = _gdn_mix(yp, norm_mix[1], jnp.zeros((B,) + state_conv_l1.shape[1:], F32),
                                 jnp.zeros((B, nvh, dh, dh), F32), *gw, nqk, tm_p)
    ys, conv_s, ssm_s = _gdn_mix(ys, norm_mix[1], state_conv_l1, state_ssm_l1, *gw, nqk, tm_s)
    yp, ys = mlp(1, yp, ys)

    mw = ((mlstm_w_in, 0), mlstm_b_i[0], mlstm_b_f[0], mlstm_norm[0], (mlstm_w_out, 0))
    zc = lambda a: jnp.zeros((B,) + a.shape[1:], F32)
    yp, c_p, n_p, m_p = _mlstm_mix(yp, norm_mix[2], zc(state_C_l2), zc(state_n_l2), zc(state_m_l2), *mw, tm_p)
    ys, c_s, n_s, m_s = _mlstm_mix(ys, norm_mix[2], state_C_l2, state_n_l2, state_m_l2, *mw, tm_s)
    yp, ys = mlp(2, yp, ys)

    yp, ys, fox3 = _fox_layer(yp, ys, cache_k_l3, cache_v_l3, cache_logf_l3, page_table, norm_mix[3],
                              (fox_w_in, 1), fox_b_f[1], fox_q_norm[1], fox_k_norm[1], (fox_w_out, 1))
    yp, ys = mlp(3, yp, ys)

    return (yp, ys) + fox0 + (conv_p, ssm_p, conv_s, ssm_s, c_p, n_p, m_p, c_s, n_s, m_s) + fox3
```

```python
import functools

import jax
import jax.numpy as jnp
from jax import lax
from jax.experimental import pallas as pl
from jax.experimental.pallas import tpu as pltpu

F32 = jnp.float32
BF16 = jnp.bfloat16
EPS = 1e-6
NEG = -1e30
LANES = 128
CHUNK = 64
VMEM_LIMIT = 56 * 1024 * 1024
ROW_TILE = 1024
COL_TILE = 512
ATTN_BLOCK = 1024
ATTN_ROWS = (256, 512)
ATTN_KEYS = (1024, 512)
DECODE_PAGES = 4
MLSTM_HEADS_PER_STEP = 8
GDN_HEADS_PER_STEP = 4
GATE_SOFTCAP = 15.0
_NT = (((1,), (1,)), ((), ()))
_TN = (((0,), (0,)), ((), ()))


def _params(*sem):
    return pltpu.CompilerParams(dimension_semantics=sem, vmem_limit_bytes=VMEM_LIMIT)


def _mm(a, b):
    return jnp.dot(a.astype(BF16), b.astype(BF16), preferred_element_type=F32)


def _mm_nt(a, b):
    return lax.dot_general(a.astype(BF16), b.astype(BF16), _NT, preferred_element_type=F32)


def _mm_tn(a, b):
    return lax.dot_general(a.astype(BF16), b.astype(BF16), _TN, preferred_element_type=F32)


def _sigmoid(x):
    return 1.0 / (1.0 + jnp.exp(-x))


def _softplus(x):
    return jnp.maximum(x, 0.0) + jnp.log(1.0 + jnp.exp(-jnp.abs(x)))


def _log_sigmoid(x):
    return -_softplus(-x)


def _rms(x, gain):
    return x * lax.rsqrt(jnp.mean(x * x, axis=-1, keepdims=True) + EPS) * gain


def _iota(shape, axis):
    return lax.broadcasted_iota(jnp.int32, shape, axis)


def _lane_cumsum(x):
    n = x.shape[-1]
    lane = _iota(x.shape, x.ndim - 1)
    s = 1
    while s < n:
        x = x + jnp.where(lane >= s, pltpu.roll(x, s, x.ndim - 1), 0.0)
        s *= 2
    return x


def _chunk_row_cumsum(x, chunk):
    pos = _iota(x.shape, 0) % chunk
    s = 1
    while s < chunk:
        x = x + jnp.where(pos >= s, pltpu.roll(x, s, 0), 0.0)
        s *= 2
    return x


def _pick_lane(block, lane_idx):
    return jnp.sum(jnp.where(_iota(block.shape, 1) == lane_idx, block, 0.0), axis=1, keepdims=True)


def _proj_kernel(x_ref, g_ref, w_ref, wt_ref, z_ref, zt_ref, xn_ref):
    @pl.when(pl.program_id(1) == 0)
    def _():
        xn = _rms(x_ref[...], g_ref[...]).astype(BF16)
        xn_ref[...] = xn
        zt_ref[...] = jnp.dot(xn, wt_ref[...].astype(BF16), preferred_element_type=F32)

    z_ref[...] = jnp.dot(xn_ref[...], w_ref[...].astype(BF16), preferred_element_type=F32)


def _layer_spec(w, block, index_map):
    layer = w[1]
    return pl.BlockSpec((None,) + block, lambda i, j: (layer,) + index_map(i, j))


def _proj(x, gain, w, N, w_tail, tm, tn):
    M, D = x.shape
    NT_ = w_tail.shape[1]
    return pl.pallas_call(
        _proj_kernel,
        grid=(M // tm, N // tn),
        in_specs=[pl.BlockSpec((tm, D), lambda i, j: (i, 0)),
                  pl.BlockSpec((1, D), lambda i, j: (0, 0)),
                  _layer_spec(w, (D, tn), lambda i, j: (0, j)),
                  pl.BlockSpec((D, NT_), lambda i, j: (0, 0))],
        out_specs=[pl.BlockSpec((tm, tn), lambda i, j: (i, j)),
                   pl.BlockSpec((tm, NT_), lambda i, j: (i, 0))],
        out_shape=[jax.ShapeDtypeStruct((M, N), F32), jax.ShapeDtypeStruct((M, NT_), F32)],
        scratch_shapes=[pltpu.VMEM((tm, D), BF16)],
        compiler_params=_params("parallel", "arbitrary"),
        name="proj",
    )(x, gain, w[0], w_tail)


def _oproj_kernel(a_ref, w_ref, r_ref, o_ref):
    o_ref[...] = r_ref[...] + jnp.dot(a_ref[...], w_ref[...].astype(BF16), preferred_element_type=F32)


def _oproj(a, w, resid, tm, tn):
    M, K = a.shape
    N = w[0].shape[2]
    return pl.pallas_call(
        _oproj_kernel,
        grid=(M // tm, N // tn),
        in_specs=[pl.BlockSpec((tm, K), lambda i, j: (i, 0)),
                  _layer_spec(w, (K, tn), lambda i, j: (0, j)),
                  pl.BlockSpec((tm, tn), lambda i, j: (i, j))],
        out_specs=pl.BlockSpec((tm, tn), lambda i, j: (i, j)),
        out_shape=jax.ShapeDtypeStruct((M, N), F32),
        compiler_params=_params("parallel", "arbitrary"),
        name="oproj",
    )(a, w[0], resid)


def _mlp_kernel(x_ref, g_ref, wu_ref, wd_ref, o_ref, xn_ref):
    @pl.when(pl.program_id(1) == 0)
    def _():
        x = x_ref[...]
        xn_ref[...] = _rms(x, g_ref[...]).astype(BF16)
        o_ref[...] = x

    h = jnp.dot(xn_ref[...], wu_ref[...].astype(BF16), preferred_element_type=F32)
    h = jnp.maximum(h, 0.0)
    o_ref[...] += jnp.dot((h * h).astype(BF16), wd_ref[...].astype(BF16), preferred_element_type=F32)


def _mlp(x, gain, w_up, w_down, tm, tf):
    M, D = x.shape
    Fdim = w_up[0].shape[2]
    return pl.pallas_call(
        _mlp_kernel,
        grid=(M // tm, Fdim // tf),
        in_specs=[pl.BlockSpec((tm, D), lambda i, j: (i, 0), pipeline_mode=pl.Buffered(1)),
                  pl.BlockSpec((1, D), lambda i, j: (0, 0)),
                  _layer_spec(w_up, (D, tf), lambda i, j: (0, j)),
                  _layer_spec(w_down, (tf, D), lambda i, j: (j, 0))],
        out_specs=pl.BlockSpec((tm, D), lambda i, j: (i, 0), pipeline_mode=pl.Buffered(1)),
        out_shape=jax.ShapeDtypeStruct((M, D), F32),
        scratch_shapes=[pltpu.VMEM((tm, D), BF16)],
        compiler_params=_params("parallel", "arbitrary"),
        name="mlp",
    )(x, gain, w_up[0], w_down[0])


def _fox_proj_kernel(x_ref, g_ref, w_ref, wt_ref, bt_ref, qg_ref, kg_ref,
                     q_ref, k_ref, kb_ref, v_ref, vb_ref, gate_ref, lf_ref, xn_ref, *, nseg, dh):
    j = pl.program_id(1)

    @pl.when(j == 0)
    def _():
        xn = _rms(x_ref[...], g_ref[...]).astype(BF16)
        xn_ref[...] = xn
        zt = jnp.dot(xn, wt_ref[...].astype(BF16), preferred_element_type=F32) + bt_ref[...]
        lf_ref[...] = _log_sigmoid(zt)

    acc = jnp.dot(xn_ref[...], w_ref[...].astype(BF16), preferred_element_type=F32)
    heads = acc.shape[1] // dh
    seg = j // nseg

    @pl.when(seg == 0)
    def _():
        for h in range(heads):
            sl = slice(h * dh, (h + 1) * dh)
            q_ref[:, sl] = _rms(acc[:, sl], qg_ref[...]).astype(BF16)

    @pl.when(seg == 1)
    def _():
        for h in range(heads):
            sl = slice(h * dh, (h + 1) * dh)
            kn = _rms(acc[:, sl], kg_ref[...])
            k_ref[:, sl] = kn
            kb_ref[:, sl] = kn.astype(BF16)

    @pl.when(seg == 2)
    def _():
        v_ref[...] = acc
        vb_ref[...] = acc.astype(BF16)

    @pl.when(seg == 3)
    def _():
        gate_ref[...] = acc


def _fox_proj(x, gain, w_main, w_tail, b_tail, q_gain, k_gain, tm, tn):
    M, D = x.shape
    dh = q_gain.shape[1]
    nseg = D // tn

    def seg_map(s):
        return lambda i, j: (i, jnp.clip(j - s * nseg, 0, nseg - 1))

    row = lambda i, j: (i, 0)
    const = lambda i, j: (0, 0)
    kern = functools.partial(_fox_proj_kernel, nseg=nseg, dh=dh)
    return pl.pallas_call(
        kern,
        grid=(M // tm, 4 * nseg),
        in_specs=[pl.BlockSpec((tm, D), row),
                  pl.BlockSpec((1, D), const),
                  _layer_spec(w_main, (D, tn), lambda i, j: (0, j)),
                  pl.BlockSpec((D, LANES), const),
                  pl.BlockSpec((1, LANES), const),
                  pl.BlockSpec((1, dh), const),
                  pl.BlockSpec((1, dh), const)],
        out_specs=[pl.BlockSpec((tm, tn), seg_map(0)),
                   pl.BlockSpec((tm, tn), seg_map(1)),
                   pl.BlockSpec((tm, tn), seg_map(1)),
                   pl.BlockSpec((tm, tn), seg_map(2)),
                   pl.BlockSpec((tm, tn), seg_map(2)),
                   pl.BlockSpec((tm, tn), seg_map(3)),
                   pl.BlockSpec((tm, LANES), row)],
        out_shape=[jax.ShapeDtypeStruct((M, D), BF16),
                   jax.ShapeDtypeStruct((M, D), F32),
                   jax.ShapeDtypeStruct((M, D), BF16),
                   jax.ShapeDtypeStruct((M, D), F32),
                   jax.ShapeDtypeStruct((M, D), BF16),
                   jax.ShapeDtypeStruct((M, D), F32),
                   jax.ShapeDtypeStruct((M, LANES), F32)],
        scratch_shapes=[pltpu.VMEM((tm, D), BF16)],
        compiler_params=_params("parallel", "arbitrary"),
        name="fox_proj",
    )(x, gain, w_main[0], w_tail, b_tail, q_gain, k_gain)


def _cumsum_kernel(x_ref, o_ref):
    o_ref[...] = _lane_cumsum(x_ref[...])


def _time_cumsum(x):
    B, H, T = x.shape
    return pl.pallas_call(
        _cumsum_kernel,
        grid=(B,),
        in_specs=[pl.BlockSpec((None, H, T), lambda b: (b, 0, 0))],
        out_specs=pl.BlockSpec((None, H, T), lambda b: (b, 0, 0)),
        out_shape=jax.ShapeDtypeStruct((B, H, T), F32),
        compiler_params=_params("parallel"),
        name="time_cumsum",
    )(x)


def _fox_attn_kernel(qi_ref, ki_ref, q_ref, k_ref, v_ref, ck_ref, g_ref, o_ref, m_ref, acc_ref,
                     *, scale, rows, keys):
    qi = qi_ref[pl.program_id(2)]
    ki = ki_ref[pl.program_id(2)]
    blk, dh = q_ref.shape

    @pl.when(ki == 0)
    def _():
        m_ref[...] = jnp.full(m_ref.shape, NEG, F32)
        acc_ref[...] = jnp.zeros(acc_ref.shape, F32)

    def update(on_diagonal):
        v_ext = jnp.concatenate([v_ref[...], jnp.ones((blk, dh), BF16)], axis=1)
        nrows, step = (rows[1], keys[1]) if on_diagonal else (rows[0], keys[0])
        for r0 in range(0, blk, nrows):
            rs = slice(r0, r0 + nrows)
            k_end = r0 + nrows if on_diagonal else blk
            m_run = m_ref[rs, :]
            acc = acc_ref[rs, :]
            for c0 in range(0, k_end, step):
                cs = slice(c0, min(c0 + step, k_end))
                s = lax.dot_general(q_ref[rs, :], k_ref[cs, :], _NT, preferred_element_type=F32) * scale
                s = s - ck_ref[:, cs]
                if on_diagonal and cs.stop > r0 + 1:
                    s = jnp.where(_iota(s.shape, 1) + c0 <= _iota(s.shape, 0) + r0, s, NEG)
                m_new = jnp.maximum(m_run, jnp.max(s, axis=1, keepdims=True))
                p = jnp.exp(s - m_new).astype(BF16)
                acc = jnp.exp(m_run - m_new) * acc + jnp.dot(p, v_ext[cs, :], preferred_element_type=F32)
                m_run = m_new
            m_ref[rs, :] = m_run
            acc_ref[rs, :] = acc

    @pl.when(ki < qi)
    def _():
        update(False)

    @pl.when(ki == qi)
    def _():
        update(True)
        o = acc_ref[:, :dh] / acc_ref[:, dh:] * _sigmoid(g_ref[...])
        o_ref[...] = o.astype(BF16)


def _fox_attn(q, kb, vb, cum_t, gate, B, T, H, dh, blk):
    M, D = q.shape
    nb = T // blk
    pairs = [(qi, ki) for qi in range(nb) for ki in range(qi + 1)]
    qi_tab = jnp.asarray([p[0] for p in pairs], jnp.int32)
    ki_tab = jnp.asarray([p[1] for p in pairs], jnp.int32)
    kern = functools.partial(_fox_attn_kernel, scale=dh ** -0.5, rows=tuple(min(r, blk) for r in ATTN_ROWS),
                             keys=tuple(min(k, blk) for k in ATTN_KEYS))
    qmap = lambda b, h, s, qt, kt: (b * nb + qt[s], h)
    kmap = lambda b, h, s, qt, kt: (b * nb + kt[s], h)
    grid_spec = pltpu.PrefetchScalarGridSpec(
        num_scalar_prefetch=2,
        grid=(B, H, len(pairs)),
        in_specs=[pl.BlockSpec((blk, dh), qmap),
                  pl.BlockSpec((blk, dh), kmap),
                  pl.BlockSpec((blk, dh), kmap),
                  pl.BlockSpec((None, 1, blk), lambda b, h, s, qt, kt: (b * H + h, 0, kt[s])),
                  pl.BlockSpec((blk, dh), qmap)],
        out_specs=pl.BlockSpec((blk, dh), qmap),
        scratch_shapes=[pltpu.VMEM((blk, 1), F32), pltpu.VMEM((blk, 2 * dh), F32)])
    return pl.pallas_call(
        kern,
        grid_spec=grid_spec,
        out_shape=jax.ShapeDtypeStruct((M, D), BF16),
        compiler_params=_params("parallel", "parallel", "arbitrary"),
        name="fox_attn",
    )(qi_tab, ki_tab, q, kb, vb, cum_t, gate)


def _fox_decode_kernel(pt_ref, q_ref, *refs, nsteps, group, nh, nt, dh, scale):
    kp_refs, vp_refs, lfp_refs = refs[:group], refs[group:2 * group], refs[2 * group:3 * group]
    kn_ref, vn_ref, lfn_ref, g_ref, o_ref, qbd_ref, m_ref, l_ref, acc_ref, coff_ref = refs[3 * group:]
    p_id = pl.program_id(1)
    R = nt * nh
    D = nh * dh

    def head_mask():
        return (_iota((R, D), 0) % nh) == (_iota((R, D), 1) // dh)

    @pl.when(p_id == 0)
    def _():
        q = q_ref[...]
        rows = jnp.concatenate([jnp.broadcast_to(q[t:t + 1, :], (nh, D)) for t in range(nt)], axis=0)
        qbd_ref[...] = jnp.where(head_mask(), rows, 0.0).astype(BF16)
        m_ref[...] = jnp.full(m_ref.shape, NEG, F32)
        l_ref[...] = jnp.zeros(l_ref.shape, F32)
        acc_ref[...] = jnp.zeros(acc_ref.shape, F32)
        coff_ref[...] = jnp.zeros(coff_ref.shape, F32)

    def step(k, v, lf_t, causal):
        tot = coff_ref[...] + _lane_cumsum(lf_t)
        bias = jnp.concatenate([tot] * nt, axis=0)
        s = lax.dot_general(qbd_ref[...], k.astype(BF16), _NT, preferred_element_type=F32) * scale - bias
        if causal:
            s = jnp.where(_iota(s.shape, 1) <= _iota(s.shape, 0) // nh, s, NEG)
        m_prev = m_ref[...]
        m_new = jnp.maximum(m_prev, jnp.max(s, axis=1, keepdims=True))
        alpha = jnp.exp(m_prev - m_new)
        p = jnp.exp(s - m_new)
        l_ref[...] = alpha * l_ref[...] + jnp.sum(p, axis=1, keepdims=True)
        acc_ref[...] = alpha * acc_ref[...] + jnp.dot(p.astype(BF16), v.astype(BF16), preferred_element_type=F32)
        m_ref[...] = m_new
        coff_ref[...] = tot[:, tot.shape[1] - 1:]

    @pl.when(p_id < nsteps)
    def _():
        page = lfp_refs[0].shape[1]
        heads = lambda ref: jnp.concatenate([ref[pl.ds(h, page, stride=nh), :] for h in range(nh)], axis=1)
        gather = lambda page_refs: jnp.concatenate([heads(r) for r in page_refs], axis=0)
        step(gather(kp_refs), gather(vp_refs), jnp.concatenate([r[...] for r in lfp_refs], axis=1), False)

    @pl.when(p_id == nsteps)
    def _():
        step(kn_ref[...], vn_ref[...], lfn_ref[...], True)
        o_full = jnp.where(head_mask(), acc_ref[...] / l_ref[...], 0.0)
        o = jnp.concatenate([jnp.sum(o_full[t * nh:(t + 1) * nh, :], axis=0, keepdims=True) for t in range(nt)],
                            axis=0)
        o_ref[...] = o * _sigmoid(g_ref[...])


def _fox_decode(page_table, q, cache_k, cache_v, cache_lf_t, k_new, v_new, lf_new_t, gate, nh, dh):
    B, nt, D = q.shape
    npages = page_table.shape[1]
    page = cache_k.shape[1] // nh
    group = DECODE_PAGES if npages % DECODE_PAGES == 0 else 1
    nsteps = npages // group
    kern = functools.partial(_fox_decode_kernel, nsteps=nsteps, group=group, nh=nh, nt=nt, dh=dh, scale=dh ** -0.5)

    def pmap(i):
        return lambda b, p, pt: (pt[b, jnp.minimum(p, nsteps - 1) * group + i], 0, 0)

    bmap = lambda b, p, pt: (b, 0, 0)
    R = nt * nh
    grid_spec = pltpu.PrefetchScalarGridSpec(
        num_scalar_prefetch=1,
        grid=(B, nsteps + 1),
        in_specs=[pl.BlockSpec((None, nt, D), bmap)]
                 + [pl.BlockSpec((None, page * nh, dh), pmap(i)) for i in range(group)]
                 + [pl.BlockSpec((None, page * nh, dh), pmap(i)) for i in range(group)]
                 + [pl.BlockSpec((None, nh, page), pmap(i)) for i in range(group)]
                 + [pl.BlockSpec((None, page, D), bmap),
                  pl.BlockSpec((None, page, D), bmap),
                  pl.BlockSpec((None, nh, page), bmap),
                  pl.BlockSpec((None, nt, D), bmap)],
        out_specs=pl.BlockSpec((None, nt, D), bmap),
        scratch_shapes=[pltpu.VMEM((R, D), BF16), pltpu.VMEM((R, 1), F32), pltpu.VMEM((R, 1), F32),
                        pltpu.VMEM((R, D), F32), pltpu.VMEM((nh, 1), F32)])
    return pl.pallas_call(
        kern,
        grid_spec=grid_spec,
        out_shape=jax.ShapeDtypeStruct((B, nt, D), F32),
        compiler_params=_params("parallel", "arbitrary"),
        name="fox_decode",
    )(page_table, q, *([cache_k] * group + [cache_v] * group + [cache_lf_t] * group), k_new, v_new, lf_new_t, gate)


def _pad_cols(a, n):
    return jnp.pad(a, ((0, 0), (0, n - a.shape[1])))


def _layer_cols(w, start):
    return w[0][w[1], :, start:]


def _row_tile(m, pref):
    return pref if m % pref == 0 else m


def _fox_layer(yp, ys, cache_k, cache_v, cache_logf, page_table, gain, w_in, b_f, q_gain, k_gain, w_out):
    B, T, D = yp.shape
    Bs, Ts, _ = ys.shape
    H = b_f.shape[0]
    dh = D // H
    w_main = w_in
    w_tail = _pad_cols(_layer_cols(w_in, 4 * D), LANES)
    b_tail = _pad_cols(b_f.reshape(1, H), LANES)
    gain = gain.reshape(1, D)
    qg = q_gain.reshape(1, dh)
    kg = k_gain.reshape(1, dh)

    x2 = yp.reshape(B * T, D)
    tm = _row_tile(B * T, ROW_TILE)
    q, k, kb, v, vb, gate, lf = _fox_proj(x2, gain, w_main, w_tail, b_tail, qg, kg, tm, COL_TILE)
    lf_p = lf[:, :H].reshape(B, T, H)
    cum_t = _time_cumsum(jnp.swapaxes(lf_p, 1, 2))
    og = _fox_attn(q, kb, vb, cum_t.reshape(B * H, 1, T), gate, B, T, H, dh, min(ATTN_BLOCK, T))
    yp_new = _oproj(og, w_out, x2, tm, COL_TILE).reshape(B, T, D)

    xs2 = ys.reshape(Bs * Ts, D)
    qs, ks, _, vs, _, gate_s, lfs = _fox_proj(xs2, gain, w_main, w_tail, b_tail, qg, kg, Bs * Ts, COL_TILE)
    n_pool, page = cache_k.shape[0], cache_k.shape[1]
    lf_s = lfs[:, :H].reshape(Bs, Ts, H)
    pad_rows = lambda a: jnp.pad(a.reshape(Bs, Ts, D), ((0, 0), (0, page - Ts), (0, 0)))
    lf_new_t = jnp.pad(jnp.swapaxes(lf_s, 1, 2), ((0, 0), (0, 0), (0, page - Ts)))
    ogs = _fox_decode(page_table, qs.astype(F32).reshape(Bs, Ts, D),
                      cache_k.reshape(n_pool, page * H, dh), cache_v.reshape(n_pool, page * H, dh),
                      jnp.swapaxes(cache_logf, 1, 2), pad_rows(ks), pad_rows(vs), lf_new_t,
                      gate_s.reshape(Bs, Ts, D), H, dh)
    ys_new = _oproj(ogs.reshape(Bs * Ts, D).astype(BF16), w_out, xs2, Bs * Ts, COL_TILE).reshape(Bs, Ts, D)

    new = (k.reshape(B, T, H, dh), v.reshape(B, T, H, dh), lf_p,
           ks.reshape(Bs, Ts, H, dh), vs.reshape(Bs, Ts, H, dh), lf_s)
    return yp_new, ys_new, new


def _gdn_gates_kernel(x_ref, al_ref, dt_ref, col_ref, row_ref, *, t_valid, nvh):
    x = x_ref[...]
    tb = x.shape[0]
    valid = pl.program_id(1) * tb + _iota(x.shape, 0) < t_valid
    beta = jnp.where(valid, _sigmoid(x), 0.0)
    g = jnp.where(valid, -jnp.exp(al_ref[...]) * _softplus(x + dt_ref[...]), 0.0)
    out = jnp.where(_iota(x.shape, 1) < nvh, beta, _chunk_row_cumsum(g, CHUNK))
    col_ref[...] = out
    row_ref[...] = out.T


def _gdn_gates(zt, a_log_l, dt_l, B, T, t_valid, nvh):
    tb = LANES
    nT = T // tb
    kern = functools.partial(_gdn_gates_kernel, t_valid=t_valid, nvh=nvh)
    return pl.pallas_call(
        kern,
        grid=(B, nT),
        in_specs=[pl.BlockSpec((tb, LANES), lambda b, t: (b * nT + t, 0)),
                  pl.BlockSpec((1, LANES), lambda b, t: (0, 0)),
                  pl.BlockSpec((1, LANES), lambda b, t: (0, 0))],
        out_specs=[pl.BlockSpec((tb, LANES), lambda b, t: (b * nT + t, 0)),
                   pl.BlockSpec((None, LANES, tb), lambda b, t: (b, 0, t))],
        out_shape=[jax.ShapeDtypeStruct((B * T, LANES), F32), jax.ShapeDtypeStruct((B, LANES, T), F32)],
        compiler_params=_params("parallel", "parallel"),
        name="gdn_gates",
    )(zt, a_log_l, dt_l)


def _gdn_conv_kernel(z_ref, w_ref, init_ref, o_ref, ext_ref, *, n_qk_tiles, dh):
    c = pl.program_id(1)
    tb = z_ref.shape[0]

    @pl.when(pl.program_id(2) == 0)
    def _():
        ext_ref[0:8, :] = init_ref[...]

    ext_ref[8:8 + tb, :] = z_ref[...]
    w = w_ref[...]
    y = (w[0:1, :] * ext_ref[5:5 + tb, :] + w[1:2, :] * ext_ref[6:6 + tb, :]
         + w[2:3, :] * ext_ref[7:7 + tb, :] + w[3:4, :] * ext_ref[8:8 + tb, :])
    y = y * _sigmoid(y)
    ext_ref[0:8, :] = ext_ref[tb:tb + 8, :]

    def l2(scale):
        for h in range(y.shape[1] // dh):
            sl = slice(h * dh, (h + 1) * dh)
            yh = y[:, sl]
            o_ref[:, sl] = yh * (lax.rsqrt(jnp.sum(yh * yh, axis=-1, keepdims=True) + EPS) * scale)

    @pl.when(c < n_qk_tiles)
    def _():
        l2(dh ** -0.5)

    @pl.when(jnp.logical_and(c >= n_qk_tiles, c < 2 * n_qk_tiles))
    def _():
        l2(1.0)

    @pl.when(c >= 2 * n_qk_tiles)
    def _():
        o_ref[...] = y


def _gdn_conv(z, conv_w, init, B, T, conv_dim, key_dim, dh, tb, tc):
    nT = T // tb
    kern = functools.partial(_gdn_conv_kernel, n_qk_tiles=key_dim // tc, dh=dh)
    return pl.pallas_call(
        kern,
        grid=(B, conv_dim // tc, nT),
        in_specs=[pl.BlockSpec((tb, tc), lambda b, c, t: (b * nT + t, c)),
                  pl.BlockSpec((conv_w.shape[0], tc), lambda b, c, t: (0, c)),
                  pl.BlockSpec((None, 8, tc), lambda b, c, t: (b, 0, c))],
        out_specs=pl.BlockSpec((tb, tc), lambda b, c, t: (b * nT + t, c)),
        out_shape=jax.ShapeDtypeStruct((B * T, conv_dim), F32),
        scratch_shapes=[pltpu.VMEM((tb + 8, tc), F32)],
        compiler_params=_params("parallel", "parallel", "arbitrary"),
        name="gdn_conv",
    )(z, conv_w, init)


def _tril_levels(n, chunk):
    i = lax.broadcasted_iota(jnp.int32, (n, n), 0)
    j = lax.broadcasted_iota(jnp.int32, (n, n), 1)
    lvl = jnp.full((n, n), -1, jnp.int32)
    s = 1
    e = 0
    while s < chunk:
        hit = ((i // (2 * s)) == (j // (2 * s))) & ((i // s) % 2 == 1) & ((j // s) % 2 == 0)
        lvl = jnp.where(hit, e, lvl)
        s *= 2
        e += 1
    return lvl


def _gdn_chunk_kernel(q_ref, k_ref, v_ref, zg_ref, col_ref, grow_ref, ng_ref, lvl_ref, s0_ref,
                      og_ref, sout_ref, s_ref, *, nvh, dh, rep):
    t = pl.program_id(2)
    L = CHUNK
    tb = q_ref.shape[0]
    nchunk = tb // L
    nhq = q_ref.shape[1] // dh

    @pl.when(t == 0)
    def _():
        s_ref[...] = s0_ref[...]

    colb = col_ref[...]
    lvl = lvl_ref[...]
    ri = _iota((tb, tb), 0)
    ci = _iota((tb, tb), 1)
    same = (ri // L) == (ci // L)
    causal = jnp.logical_and(same, ci <= ri)
    strict = jnp.logical_and(same, ci < ri)
    eye = (ri == ci).astype(F32)

    heads = range(nhq)
    qs = [q_ref[:, hh * dh:(hh + 1) * dh] for hh in heads]
    ks = [k_ref[:, hh * dh:(hh + 1) * dh] for hh in heads]
    vheads = range(nhq * rep)
    g_cols, qkd, m_full, rhs = [], [], [], []
    for hh in heads:
        hq = pl.program_id(1) * nhq + hh
        kk = _mm_nt(ks[hh], ks[hh])
        qk = _mm_nt(qs[hh], ks[hh])
        for r in range(rep):
            vh = hh * rep + r
            beta_c = _pick_lane(colb, rep * hq + r)
            g_c = _pick_lane(colb, nvh + rep * hq + r)
            g_r = grow_ref[vh:vh + 1, :]
            decay = jnp.where(causal, jnp.exp(jnp.where(causal, g_c - g_r, 0.0)), 0.0)
            m_full.append(jnp.where(strict, beta_c * kk * decay, 0.0))
            qkd.append(qk * decay)
            g_cols.append(g_c)
            kb = ks[hh] * beta_c
            rhs.append(jnp.concatenate([v_ref[:, vh * dh:(vh + 1) * dh] * beta_c, kb * jnp.exp(g_c)], axis=1))

    tinv = [eye - jnp.where(lvl == 0, m_full[vh], 0.0) for vh in vheads]
    s, e = 2, 1
    while s < L:
        half = [_mm(tinv[vh], jnp.where(lvl == e, m_full[vh], 0.0)) for vh in vheads]
        tinv = [tinv[vh] - _mm(half[vh], tinv[vh]) for vh in vheads]
        s *= 2
        e += 1
    sol = [_mm(tinv[vh], rhs[vh]) for vh in vheads]

    for c in range(nchunk):
        sl = slice(c * L, (c + 1) * L)
        for vh in range(nhq * rep):
            hh, r = vh // rep, vh % rep
            g_c = g_cols[vh][sl]
            u = sol[vh][sl, :dh]
            w = sol[vh][sl, dh:]
            S = s_ref[vh]
            v_new = u - _mm(w, S)
            o = _mm(qs[hh][sl] * jnp.exp(g_c), S) + _mm(qkd[vh][sl, sl], v_new)
            g_last = g_c[L - 1:L, :]
            k_dec = ks[hh][sl] * jnp.exp(g_last - g_c)
            s_ref[vh] = S * jnp.exp(g_last) + _mm_tn(k_dec, v_new)
            zg = zg_ref[sl, vh * dh:(vh + 1) * dh]
            og_ref[sl, vh * dh:(vh + 1) * dh] = (_rms(o, ng_ref[...]) * (zg * _sigmoid(zg))).astype(BF16)

    @pl.when(t == pl.num_programs(2) - 1)
    def _():
        sout_ref[...] = s_ref[...]


def _gdn_chunk(qkv, z, gcol, grow, norm_g, s0, B, T, nqk, nvh, dh):
    tb = LANES
    nT = T // tb
    rep = nvh // nqk
    nhq = GDN_HEADS_PER_STEP
    qw, vw = nhq * dh, nhq * rep * dh
    key_dim = nqk * dh
    grow4 = grow.reshape(B, LANES // (nhq * rep), nhq * rep, T)
    kern = functools.partial(_gdn_chunk_kernel, nvh=nvh, dh=dh, rep=rep)
    row = lambda off: (lambda b, h, t: (b * nT + t, off + h))
    return pl.pallas_call(
        kern,
        grid=(B, nqk // nhq, nT),
        in_specs=[pl.BlockSpec((tb, qw), row(0)),
                  pl.BlockSpec((tb, qw), row(key_dim // qw)),
                  pl.BlockSpec((tb, vw), row(2 * key_dim // vw)),
                  pl.BlockSpec((tb, vw), row((2 * key_dim + nvh * dh) // vw)),
                  pl.BlockSpec((tb, LANES), lambda b, h, t: (b * nT + t, 0)),
                  pl.BlockSpec((None, None, nhq * rep, tb), lambda b, h, t: (b, nvh // (nhq * rep) + h, 0, t)),
                  pl.BlockSpec((1, dh), lambda b, h, t: (0, 0)),
                  pl.BlockSpec((tb, tb), lambda b, h, t: (0, 0)),
                  pl.BlockSpec((None, nhq * rep, dh, dh), lambda b, h, t: (b, h, 0, 0))],
        out_specs=[pl.BlockSpec((tb, vw), lambda b, h, t: (b * nT + t, h)),
                   pl.BlockSpec((None, nhq * rep, dh, dh), lambda b, h, t: (b, h, 0, 0))],
        out_shape=[jax.ShapeDtypeStruct((B * T, nvh * dh), BF16),
                   jax.ShapeDtypeStruct((B, nvh, dh, dh), F32)],
        scratch_shapes=[pltpu.VMEM((nhq * rep, dh, dh), F32)],
        compiler_params=_params("parallel", "parallel", "arbitrary"),
        name="gdn_chunk",
    )(qkv, qkv, qkv, z, gcol, grow4, norm_g, _tril_levels(tb, CHUNK), s0)


def _pad_time(a, B, T, Tp):
    if Tp == T:
        return a
    return jnp.pad(a.reshape(B, T, -1), ((0, 0), (0, Tp - T), (0, 0))).reshape(B * Tp, -1)


def _unpad_time(a, B, T, Tp):
    if Tp == T:
        return a
    return a.reshape(B, Tp, -1)[:, :T].reshape(B * T, -1)


def _gdn_mix(y, gain, conv_state, s0, w_in, conv_w, a_log, dt_bias, norm_g, w_out, nqk, tm):
    B, T, D = y.shape
    nvh = a_log.shape[0]
    dh = norm_g.shape[0]
    key_dim = nqk * dh
    conv_dim = 2 * key_dim + nvh * dh
    n_main = conv_dim + nvh * dh
    x2 = y.reshape(B * T, D)
    z, zt = _proj(x2, gain.reshape(1, D), w_in, n_main, _pad_cols(_layer_cols(w_in, n_main), LANES), tm, COL_TILE)
    Tp = -(-T // LANES) * LANES
    zp, ztp = _pad_time(z, B, T, Tp), _pad_time(zt, B, T, Tp)
    lane_par = lambda p: jnp.pad(p.reshape(1, nvh), ((0, 0), (nvh, LANES - 2 * nvh)))
    gcol, grow = _gdn_gates(ztp, lane_par(a_log), lane_par(dt_bias), B, Tp, T, nvh)
    init = jnp.pad(conv_state, ((0, 0), (8 - conv_state.shape[1], 0), (0, 0)))
    qkv = _gdn_conv(zp, conv_w, init, B, Tp, conv_dim, key_dim, dh, min(256, Tp), COL_TILE)
    og, s_new = _gdn_chunk(qkv, zp, gcol, grow, norm_g.reshape(1, dh), s0, B, Tp, nqk, nvh, dh)
    og = _unpad_time(og, B, T, Tp)
    y_new = _oproj(og, w_out, x2, tm, COL_TILE).reshape(B, T, D)
    keep = conv_state.shape[1]
    n_last = min(T, keep)
    last = jnp.stack([lax.slice(z, (b * T + T - n_last, 0), (b * T + T, conv_dim)) for b in range(B)])
    return y_new, jnp.concatenate([conv_state, last], axis=1)[:, -keep:], s_new


def _mlstm_gates_kernel(xi_ref, xf_ref, bi_ref, bf_ref, col_ref, row_ref, *, t_valid):
    xi = xi_ref[...]
    tb = xi.shape[0]
    valid = pl.program_id(1) * tb + _iota(xi.shape, 0) < t_valid
    ig = GATE_SOFTCAP * jnp.tanh((xi + bi_ref[...]) / GATE_SOFTCAP)
    lf = _log_sigmoid(GATE_SOFTCAP * jnp.tanh((xf_ref[...] + bf_ref[...]) / GATE_SOFTCAP))
    ig = jnp.where(valid, ig, NEG)
    a = _chunk_row_cumsum(jnp.where(valid, lf, 0.0), CHUNK)
    ia = ig - a
    col_ref[:, :LANES] = a
    col_ref[:, LANES:] = ia
    row_ref[...] = ia.T


def _mlstm_gates(zt, b_i_l, b_f_l, B, T, t_valid):
    tb = LANES
    nT = T // tb
    kern = functools.partial(_mlstm_gates_kernel, t_valid=t_valid)
    return pl.pallas_call(
        kern,
        grid=(B, nT),
        in_specs=[pl.BlockSpec((tb, LANES), lambda b, t: (b * nT + t, 0)),
                  pl.BlockSpec((tb, LANES), lambda b, t: (b * nT + t, 1)),
                  pl.BlockSpec((1, LANES), lambda b, t: (0, 0)),
                  pl.BlockSpec((1, LANES), lambda b, t: (0, 0))],
        out_specs=[pl.BlockSpec((tb, 2 * LANES), lambda b, t: (b * nT + t, 0)),
                   pl.BlockSpec((None, LANES, tb), lambda b, t: (b, 0, t))],
        out_shape=[jax.ShapeDtypeStruct((B * T, 2 * LANES), F32), jax.ShapeDtypeStruct((B, LANES, T), F32)],
        compiler_params=_params("parallel", "parallel"),
        name="mlstm_gates",
    )(zt, zt, b_i_l, b_f_l)


def _mlstm_chunk_kernel(q_ref, k_ref, v_ref, og_ref, col_ref, row_ref, ng_ref, c0_ref, n0_ref, m0_ref,
                        out_ref, cout_ref, nout_ref, mout_ref, c_ref, n_ref, m_ref, *, dqk, dv):
    t = pl.program_id(2)
    L = CHUNK
    tb = q_ref.shape[0]
    nhs = c_ref.shape[0]

    @pl.when(t == 0)
    def _():
        c_ref[...] = c0_ref[...]
        n_ref[...] = n0_ref[...]
        m_ref[...] = m0_ref[...]

    causal = _iota((L, L), 1) <= _iota((L, L), 0)
    for c in range(tb // L):
        sl = slice(c * L, (c + 1) * L)
        for hh in range(nhs):
            h = pl.program_id(1) * nhs + hh
            qs = slice(hh * dqk, (hh + 1) * dqk)
            vs = slice(hh * dv, (hh + 1) * dv)
            q = q_ref[sl, qs]
            k = k_ref[sl, qs] * (dqk ** -0.5)
            v = v_ref[sl, vs]
            a_c = _pick_lane(col_ref[sl, :LANES], h)
            ia_c = _pick_lane(col_ref[sl, LANES:], h)
            ia_r = row_ref[hh:hh + 1, sl]
            dmat = jnp.where(causal, a_c + ia_r, NEG)
            dmax = jnp.max(dmat, axis=1, keepdims=True)
            m = m_ref[hh]
            inter = a_c + m
            m_row = jnp.maximum(inter, dmax)
            w_inter = jnp.exp(inter - m_row)
            smat = _mm_nt(q, k) * jnp.exp(dmat - m_row)
            C = c_ref[hh]
            nvec = n_ref[hh]
            num = w_inter * _mm(q, C) + _mm(smat, v)
            den = w_inter * jnp.sum(q * nvec, axis=1, keepdims=True) + jnp.sum(smat, axis=1, keepdims=True)
            hid = num / jnp.maximum(jnp.abs(den), jnp.exp(-m_row))
            m_new = m_row[L - 1:L, :]
            a_last = a_c[L - 1:L, :]
            scale = jnp.exp(a_last + m - m_new)
            kw = k * jnp.exp(a_last + ia_c - m_new)
            c_ref[hh] = scale * C + _mm_tn(kw, v)
            n_ref[hh] = scale * nvec + jnp.sum(kw, axis=0, keepdims=True)
            m_ref[hh] = m_new
            out_ref[sl, vs] = (_rms(hid, ng_ref[hh:hh + 1, :]) * _sigmoid(og_ref[sl, vs])).astype(BF16)

    @pl.when(t == pl.num_programs(2) - 1)
    def _():
        cout_ref[...] = c_ref[...]
        nout_ref[...] = n_ref[...]
        mout_ref[...] = m_ref[...]


def _mlstm_chunk(z, gcol, grow, norm_g, c0, n0, m0, B, T, nh, dqk, dv):
    tb = LANES
    nT = T // tb
    nhs = MLSTM_HEADS_PER_STEP
    qw, vw = nhs * dqk, nhs * dv
    kern = functools.partial(_mlstm_chunk_kernel, dqk=dqk, dv=dv)
    row = lambda off: (lambda b, h, t: (b * nT + t, off + h))
    st = lambda b, h, t: (b, h, 0, 0)
    return pl.pallas_call(
        kern,
        grid=(B, nh // nhs, nT),
        in_specs=[pl.BlockSpec((tb, qw), row(0)),
                  pl.BlockSpec((tb, qw), row(nh * dqk // qw)),
                  pl.BlockSpec((tb, vw), row(2 * nh * dqk // vw)),
                  pl.BlockSpec((tb, vw), row((2 * nh * dqk + nh * dv) // vw)),
                  pl.BlockSpec((tb, 2 * LANES), lambda b, h, t: (b * nT + t, 0)),
                  pl.BlockSpec((None, None, nhs, tb), lambda b, h, t: (b, h, 0, t)),
                  pl.BlockSpec((None, nhs, dv), lambda b, h, t: (h, 0, 0)),
                  pl.BlockSpec((None, nhs, dqk, dv), st),
                  pl.BlockSpec((None, nhs, 1, dqk), st),
                  pl.BlockSpec((None, nhs, 1, 1), st)],
        out_specs=[pl.BlockSpec((tb, vw), lambda b, h, t: (b * nT + t, h)),
                   pl.BlockSpec((None, nhs, dqk, dv), st),
                   pl.BlockSpec((None, nhs, 1, dqk), st),
                   pl.BlockSpec((None, nhs, 1, 1), st)],
        out_shape=[jax.ShapeDtypeStruct((B * T, nh * dv), BF16),
                   jax.ShapeDtypeStruct((B, nh, dqk, dv), F32),
                   jax.ShapeDtypeStruct((B, nh, 1, dqk), F32),
                   jax.ShapeDtypeStruct((B, nh, 1, 1), F32)],
        scratch_shapes=[pltpu.VMEM((nhs, dqk, dv), F32), pltpu.VMEM((nhs, 1, dqk), F32),
                        pltpu.VMEM((nhs, 1, 1), F32)],
        compiler_params=_params("parallel", "parallel", "arbitrary"),
        name="mlstm_chunk",
    )(z, z, z, z, gcol, grow.reshape(B, LANES // nhs, nhs, T), norm_g.reshape(nh // nhs, nhs, dv), c0, n0, m0)


def _mlstm_mix(y, gain, c0, n0, m0, w_in, b_i, b_f, norm_g, w_out, tm):
    B, T, D = y.shape
    nh = b_i.shape[0]
    dqk = c0.shape[2]
    dv = c0.shape[3]
    n_main = 2 * nh * dqk + nh * dv + D
    x2 = y.reshape(B * T, D)
    w_gate = _layer_cols(w_in, n_main)
    w_tail = jnp.concatenate([_pad_cols(w_gate[:, :nh], LANES), _pad_cols(w_gate[:, nh:], LANES)], axis=1)
    z, zt = _proj(x2, gain.reshape(1, D), w_in, n_main, w_tail, tm, COL_TILE)
    Tp = -(-T // LANES) * LANES
    zp, ztp = _pad_time(z, B, T, Tp), _pad_time(zt, B, T, Tp)
    gcol, grow = _mlstm_gates(ztp, _pad_cols(b_i.reshape(1, nh), LANES), _pad_cols(b_f.reshape(1, nh), LANES),
                              B, Tp, T)
    og, c_new, n_new, m_new = _mlstm_chunk(zp, gcol, grow, norm_g.reshape(nh, 1, dv), c0,
                                           n0.reshape(B, nh, 1, dqk), m0.reshape(B, nh, 1, 1), B, Tp, nh, dqk, dv)
    og = _unpad_time(og, B, T, Tp)
    y_new = _oproj(og, w_out, x2, tm, COL_TILE).reshape(B, T, D)
    return y_new, c_new, n_new.reshape(B, nh, dqk), m_new.reshape(B, nh)


def _mlp_layer(y, gain, w_up, w_down, tm):
    B, T, D = y.shape
    return _mlp(y.reshape(B * T, D), gain.reshape(1, D), w_up, w_down, tm, COL_TILE).reshape(B, T, D)


def kernel(x_prompt, x_sample, cache_k_l0, cache_v_l0, cache_logf_l0, state_conv_l1, state_ssm_l1, state_C_l2, state_n_l2, state_m_l2, cache_k_l3, cache_v_l3, cache_logf_l3, page_table, norm_mix, norm_mlp, w_up, w_down, fox_w_in, fox_b_f, fox_q_norm, fox_k_norm, fox_w_out, gdn_w_in, gdn_conv_w, gdn_A_log, gdn_dt_bias, gdn_norm, gdn_w_out, mlstm_w_in, mlstm_b_i, mlstm_b_f, mlstm_norm, mlstm_w_out):
    yp, ys = x_prompt, x_sample
    B, T, D = yp.shape
    Bs, Ts, _ = ys.shape
    tm_p = _row_tile(B * T, ROW_TILE)
    tm_s = Bs * Ts
    nqk = D // gdn_norm.shape[1]

    def mlp(i, yp, ys):
        return (_mlp_layer(yp, norm_mlp[i], (w_up, i), (w_down, i), tm_p),
                _mlp_layer(ys, norm_mlp[i], (w_up, i), (w_down, i), tm_s))

    yp, ys, fox0 = _fox_layer(yp, ys, cache_k_l0, cache_v_l0, cache_logf_l0, page_table, norm_mix[0],
                              (fox_w_in, 0), fox_b_f[0], fox_q_norm[0], fox_k_norm[0], (fox_w_out, 0))
    yp, ys = mlp(0, yp, ys)

    gw = ((gdn_w_in, 0), gdn_conv_w[0], gdn_A_log[0], gdn_dt_bias[0], gdn_norm[0], (gdn_w_out, 0))
    nvh, dh = state_ssm_l1.shape[1], state_ssm_l1.shape[2]
    yp, conv_p, ssm_p = _gdn_mix(yp, norm_mix[1], jnp.zeros((B,) + state_conv_l1.shape[1:], F32),
                                 jnp.zeros((B, nvh, dh, dh), F32), *gw, nqk, tm_p)
    ys, conv_s, ssm_s = _gdn_mix(ys, norm_mix[1], state_conv_l1, state_ssm_l1, *gw, nqk, tm_s)
    yp, ys = mlp(1, yp, ys)

    mw = ((mlstm_w_in, 0), mlstm_b_i[0], mlstm_b_f[0], mlstm_norm[0], (mlstm_w_out, 0))
    zc = lambda a: jnp.zeros((B,) + a.shape[1:], F32)
    yp, c_p, n_p, m_p = _mlstm_mix(yp, norm_mix[2], zc(state_C_l2), zc(state_n_l2), zc(state_m_l2), *mw, tm_p)
    ys, c_s, n_s, m_s = _mlstm_mix(ys, norm_mix[2], state_C_l2, state_n_l2, state_m_l2, *mw, tm_s)
    yp, ys = mlp(2, yp, ys)

    yp, ys, fox3 = _fox_layer(yp, ys, cache_k_l3, cache_v_l3, cache_logf_l3, page_table, norm_mix[3],
                              (fox_w_in, 1), fox_b_f[1], fox_q_norm[1], fox_k_norm[1], (fox_w_out, 1))
    yp, ys = mlp(3, yp, ys)

    return (yp, ys) + fox0 + (conv_p, ssm_p, conv_s, ssm_s, c_p, n_p, m_p, c_s, n_s, m_s) + fox3
```
